```python
import jax, jax.numpy as jnp
from jax import lax
import numpy as np

D_MODEL = 2048
BATCH = 8
SEQ = 2048
DEPTH = 1

CHUNK = 64
GMLP_WIDTH = D_MODEL // 2
RWKV_WIDTH = D_MODEL - GMLP_WIDTH
GMLP_BLOCK = 128
GMLP_HEADS = 8
GMLP_HEAD_DIM = GMLP_WIDTH // GMLP_HEADS
RWKV_HEAD_DIM = 64
RWKV_HEADS = RWKV_WIDTH // RWKV_HEAD_DIM
DECAY_LORA = 64
AAA_LORA = 64
GATE_LORA = 160
RWKV_IN = 3 * RWKV_WIDTH + DECAY_LORA + AAA_LORA + GATE_LORA
IN_WIDTH = 2 * GMLP_WIDTH + RWKV_IN
PEER_KEYS = 128
PEER_EXPERTS = PEER_KEYS * PEER_KEYS
PEER_HEADS = 8
PEER_TOPK = 16
PEER_QK_HALF = 128
PEER_TOKEN_BLOCK = 128
N_MOD = 6
NORM_EPS = 1e-6
LN_EPS = 1e-5
GROUPNORM_EPS = 64e-5

kernel_name = "hybrid_gmlp_rwkv7_peer_adaln_block"


def rms_norm(x, g):
    xf = x.astype(jnp.float32)
    y = xf * lax.rsqrt(jnp.mean(xf * xf, axis=-1, keepdims=True) + NORM_EPS)
    return (y * g.astype(jnp.float32)).astype(x.dtype)


def layer_norm(x, g, b, eps):
    xf = x.astype(jnp.float32)
    mu = jnp.mean(xf, axis=-1, keepdims=True)
    var = jnp.mean(jnp.square(xf - mu), axis=-1, keepdims=True)
    y = (xf - mu) * lax.rsqrt(var + eps)
    return (y * g.astype(jnp.float32) + b.astype(jnp.float32)).astype(x.dtype)


def modulate(h, shift, scale):
    return h * (1 + scale[:, None, :]) + shift[:, None, :]


def gmlp_spatial_gate(u, v, v_g, v_b, w_s, b_s):
    bsz, seq, _ = u.shape
    vn = layer_norm(v, v_g, v_b, LN_EPS)
    chunk_id = jnp.arange(GMLP_BLOCK) // CHUNK
    mask = (chunk_id[None, :] <= chunk_id[:, None]).astype(w_s.dtype)
    vb = vn.reshape(bsz, seq // GMLP_BLOCK, GMLP_BLOCK, GMLP_HEADS, GMLP_HEAD_DIM)
    z = jnp.einsum("hij,bnjhd->bnihd", w_s * mask, vb) + b_s.T[None, None, :, :, None]
    return u * z.reshape(bsz, seq, GMLP_WIDTH)


def token_shift(p):
    return jnp.pad(p, ((0, 0), (1, 0), (0, 0)))[:, :-1]


def wkv7_scan(r, decay, k, v, a_vec, b_vec):
    bsz, _, nh, n = r.shape
    xs = tuple(jnp.moveaxis(t, 1, 0) for t in (r, decay, k, v, a_vec, b_vec))

    def step(state, inp):
        r_t, w_t, k_t, v_t, a_t, b_t = inp
        sa = jnp.einsum("bhvk,bhk->bhv", state, a_t)
        state = (state * w_t[:, :, None, :]
                 + sa[..., None] * b_t[:, :, None, :]
                 + v_t[..., None] * k_t[:, :, None, :])
        y_t = jnp.einsum("bhvk,bhk->bhv", state, r_t)
        return state, y_t

    s0 = jnp.zeros((bsz, nh, n, n), jnp.float32)
    _, ys = lax.scan(step, s0, xs)
    return jnp.moveaxis(ys, 0, 1)


def rwkv7_time_mix(p, mu, w0, w2, a0, a2, g2, k_k, k_a, r_k, ln_g, ln_b):
    bsz, seq, _ = p.shape
    f32 = jnp.float32
    p = p + (token_shift(p) - p) * mu
    cuts = [RWKV_WIDTH, 2 * RWKV_WIDTH, 3 * RWKV_WIDTH,
            3 * RWKV_WIDTH + DECAY_LORA, 3 * RWKV_WIDTH + DECAY_LORA + AAA_LORA]
    r, k, v, wl, al, gl = jnp.split(p, cuts, axis=-1)
    w = -jax.nn.softplus(-(w0 + jnp.tanh(wl) @ w2)) - 0.5
    a = jax.nn.sigmoid(a0 + al @ a2)
    g = jax.nn.sigmoid(gl) @ g2

    def heads(t):
        return t.reshape(bsz, seq, RWKV_HEADS, RWKV_HEAD_DIM).astype(f32)

    kk = heads(k * k_k)
    kk = kk / jnp.maximum(jnp.sqrt(jnp.sum(kk * kk, axis=-1, keepdims=True)), 1e-12)
    k = k * (1 + (a - 1) * k_a)
    rh, kh, vh, ah = heads(r), heads(k), heads(v), heads(a)
    decay = jnp.exp(-jnp.exp(heads(w)))
    y = wkv7_scan(rh, decay, kh, vh, -kk, kk * ah)
    y = layer_norm(y, ln_g.reshape(RWKV_HEADS, RWKV_HEAD_DIM),
                   ln_b.reshape(RWKV_HEADS, RWKV_HEAD_DIM), GROUPNORM_EPS)
    bonus = jnp.sum(rh * kh * r_k.astype(f32), axis=-1, keepdims=True) * vh
    return (y + bonus).reshape(bsz, seq, RWKV_WIDTH).astype(p.dtype) * g


def peer_ffn(h, w_q, sub_keys, expert_u, expert_v):
    bsz, seq, d = h.shape
    f32 = jnp.float32
    q = (h @ w_q).reshape(bsz, seq, PEER_HEADS, 2, PEER_QK_HALF).astype(f32)
    scores = jnp.einsum("bshpd,hpnd->bshpn", q, sub_keys.astype(f32))
    s_top, i_top = lax.top_k(scores, PEER_TOPK)
    cand = s_top[..., 0, :, None] + s_top[..., 1, None, :]
    cand_idx = i_top[..., 0, :, None] * PEER_KEYS + i_top[..., 1, None, :]
    cand = cand.reshape(bsz, seq, PEER_HEADS, PEER_TOPK * PEER_TOPK)
    cand_idx = cand_idx.reshape(bsz, seq, PEER_HEADS, PEER_TOPK * PEER_TOPK)
    s_fin, pos = lax.top_k(cand, PEER_TOPK)
    e_idx = jnp.take_along_axis(cand_idx, pos, axis=-1)
    gate = jax.nn.softmax(s_fin, axis=-1).astype(h.dtype)

    n_sel = PEER_HEADS * PEER_TOPK
    n_blocks = (bsz * seq) // PEER_TOKEN_BLOCK
    hb = h.reshape(n_blocks, PEER_TOKEN_BLOCK, d)
    eb = e_idx.reshape(n_blocks, PEER_TOKEN_BLOCK, n_sel)
    gb = gate.reshape(n_blocks, PEER_TOKEN_BLOCK, n_sel)

    def block(args):
        hx, ei, gi = args
        u_sel = expert_u[ei]
        act = jax.nn.gelu(jnp.einsum("tkd,td->tk", u_sel, hx), approximate=False) * gi
        return jnp.einsum("tk,tkd->td", act, expert_v[ei])

    out = lax.map(block, (hb, eb, gb))
    return out.reshape(bsz, seq, d)


def setup_inputs(seed: int = 0) -> dict:
    key = jax.random.key(seed)
    ks = jax.random.split(key, 32)
    f32 = jnp.float32
    L, D = DEPTH, D_MODEL

    def nrm(k, shape, std):
        return jax.random.normal(k, shape, f32) * std

    def gain(k, shape):
        return 1.0 + nrm(k, shape, 0.02)

    return {
        "x": nrm(ks[0], (BATCH, SEQ, D), 1.0),
        "c": nrm(ks[1], (BATCH, D), 1.0),
        "ada_w": nrm(ks[2], (L, D, N_MOD * D), 0.5 * D ** -0.5),
        "ada_b": nrm(ks[3], (L, N_MOD * D), 0.01),
        "norm1_g": gain(ks[4], (L, D)),
        "w_in": nrm(ks[5], (L, D, IN_WIDTH), D ** -0.5),
        "gmlp_v_g": gain(ks[6], (L, GMLP_WIDTH)),
        "gmlp_v_b": nrm(ks[7], (L, GMLP_WIDTH), 0.02),
        "gmlp_ws": nrm(ks[8], (L, GMLP_HEADS, GMLP_BLOCK, GMLP_BLOCK), GMLP_BLOCK ** -0.5),
        "gmlp_bs": gain(ks[9], (L, GMLP_HEADS, GMLP_BLOCK)),
        "gmlp_out_g": gain(ks[10], (L, GMLP_WIDTH)),
        "rwkv_mu": jax.random.uniform(ks[11], (L, RWKV_IN), f32),
        "rwkv_w0": jax.random.uniform(ks[12], (L, RWKV_WIDTH), f32, -6.5, -1.5),
        "rwkv_w2": nrm(ks[13], (L, DECAY_LORA, RWKV_WIDTH), 0.1 * DECAY_LORA ** -0.5),
        "rwkv_a0": nrm(ks[14], (L, RWKV_WIDTH), 0.1),
        "rwkv_a2": nrm(ks[15], (L, AAA_LORA, RWKV_WIDTH), 0.1 * AAA_LORA ** -0.5),
        "rwkv_g2": nrm(ks[16], (L, GATE_LORA, RWKV_WIDTH), GATE_LORA ** -0.5),
        "rwkv_kk": 0.85 + nrm(ks[17], (L, RWKV_WIDTH), 0.02),
        "rwkv_ka": gain(ks[18], (L, RWKV_WIDTH)),
        "rwkv_rk": nrm(ks[19], (L, RWKV_HEADS, RWKV_HEAD_DIM), 0.1),
        "rwkv_ln_g": gain(ks[20], (L, RWKV_WIDTH)),
        "rwkv_ln_b": nrm(ks[21], (L, RWKV_WIDTH), 0.02),
        "w_out": nrm(ks[22], (L, D, D), D ** -0.5),
        "norm2_g": gain(ks[23], (L, D)),
        "peer_wq": nrm(ks[24], (L, D, PEER_HEADS * 2 * PEER_QK_HALF), D ** -0.5),
        "peer_keys": nrm(ks[25], (L, PEER_HEADS, 2, PEER_KEYS, PEER_QK_HALF), PEER_QK_HALF ** -0.5),
        "peer_u": nrm(ks[26], (L, PEER_EXPERTS, D), D ** -0.5),
        "peer_v": nrm(ks[27], (L, PEER_EXPERTS, D), PEER_HEADS ** -0.5),
        "final_g": gain(ks[28], (D,)),
    }


def reference(x, c, ada_w, ada_b, norm1_g, w_in, gmlp_v_g, gmlp_v_b, gmlp_ws, gmlp_bs,
              gmlp_out_g, rwkv_mu, rwkv_w0, rwkv_w2, rwkv_a0, rwkv_a2, rwkv_g2, rwkv_kk,
              rwkv_ka, rwkv_rk, rwkv_ln_g, rwkv_ln_b, w_out, norm2_g, peer_wq, peer_keys,
              peer_u, peer_v, final_g):
    for l in range(DEPTH):
        mod = jax.nn.silu(c) @ ada_w[l] + ada_b[l]
        sh1, sc1, gt1, sh2, sc2, gt2 = jnp.split(mod, N_MOD, axis=-1)

        h = modulate(rms_norm(x, norm1_g[l]), sh1, sc1)
        p = h @ w_in[l]
        u, v, p_rwkv = jnp.split(p, [GMLP_WIDTH, 2 * GMLP_WIDTH], axis=-1)
        y_a = gmlp_spatial_gate(jax.nn.gelu(u, approximate=False), jax.nn.gelu(v, approximate=False),
                                gmlp_v_g[l], gmlp_v_b[l], gmlp_ws[l], gmlp_bs[l])
        y_a = rms_norm(y_a, gmlp_out_g[l])
        y_b = rwkv7_time_mix(p_rwkv, rwkv_mu[l], rwkv_w0[l], rwkv_w2[l], rwkv_a0[l], rwkv_a2[l],
                             rwkv_g2[l], rwkv_kk[l], rwkv_ka[l], rwkv_rk[l], rwkv_ln_g[l], rwkv_ln_b[l])
        y = jnp.concatenate([y_a, y_b], axis=-1) @ w_out[l]
        x = x + gt1[:, None, :] * y

        h = modulate(rms_norm(x, norm2_g[l]), sh2, sc2)
        x = x + gt2[:, None, :] * peer_ffn(h, peer_wq[l], peer_keys[l], peer_u[l], peer_v[l])
    return rms_norm(x, final_g)
```

```python
import functools

import jax
import jax.numpy as jnp
from jax import lax
from jax.experimental import pallas as pl
from jax.experimental.pallas import tpu as pltpu

F32 = jnp.float32
BF16 = jnp.bfloat16
I32 = jnp.int32
HI = lax.Precision.HIGHEST

D_MODEL = 2048
GMLP_WIDTH = 1024
RWKV_WIDTH = 1024
GMLP_BLOCK = 128
GMLP_HEADS = 8
CHUNK = 64
HEAD_DIM = 64
DECAY_LORA = 64
AAA_LORA = 64
GATE_LORA = 160
LORA_PAD = 512
IN_WIDTH = 5408
IN_PAD = 5632
PEER_KEYS = 128
PEER_HEADS = 8
PEER_TOPK = 16
N_SEL = PEER_HEADS * PEER_TOPK
N_EXPERTS = PEER_KEYS * PEER_KEYS
N_STRIPS = 8
STRIP = N_EXPERTS // N_STRIPS
STRIP_SHIFT = STRIP.bit_length() - 1
NORM_EPS = 1e-6
LN_EPS = 1e-5
GROUPNORM_EPS = 64e-5
SCAN_CHUNK = 64
LANES = 128
SUBLANES = 8
TOK_TILE = 64
SUB_TOK = 8
ROUND_SLOTS = 8
VMEM_LIMIT = 56 * 1024 * 1024


def _cparams(sem, vmem=None):
    return pltpu.CompilerParams(dimension_semantics=sem, vmem_limit_bytes=vmem)


def _gelu(x):
    return 0.5 * x * (1.0 + lax.erf(x * 0.7071067811865476))


def _dot(a, b, **kw):
    return jnp.dot(a, b, preferred_element_type=F32, **kw)


def _dot_nt(a, b, **kw):
    return lax.dot_general(a, b, (((1,), (1,)), ((), ())), preferred_element_type=F32, **kw)


def _dot_tn(a, b, **kw):
    return lax.dot_general(a, b, (((0,), (0,)), ((), ())), preferred_element_type=F32, **kw)


def _mod_kernel(c_ref, w_ref, b_ref, o_ref):
    c = c_ref[...]
    sc = (c * jax.nn.sigmoid(c)).astype(BF16)
    o_ref[...] = _dot(sc, w_ref[...].astype(BF16)) + b_ref[...]


def _mod(c, w, b):
    bsz, d = c.shape
    n = w.shape[1]
    tn = 1024
    return pl.pallas_call(
        _mod_kernel,
        grid=(n // tn,),
        in_specs=[pl.BlockSpec((bsz, d), lambda j: (0, 0)),
                  pl.BlockSpec((d, tn), lambda j: (0, j)),
                  pl.BlockSpec((1, tn), lambda j: (0, j))],
        out_specs=pl.BlockSpec((bsz, tn), lambda j: (0, j)),
        out_shape=jax.ShapeDtypeStruct((bsz, n), F32),
        compiler_params=_cparams(("arbitrary",), VMEM_LIMIT),
        name="mod",
    )(c, w, b.reshape(1, n))


def _in_proj_kernel(x_ref, g_ref, sh_ref, sc_ref, w_ref, o_ref, h_scr):
    @pl.when(pl.program_id(1) == 0)
    def _():
        x = x_ref[...]
        y = x * lax.rsqrt(jnp.mean(x * x, axis=-1, keepdims=True) + NORM_EPS) * g_ref[...]
        h_scr[...] = (y * (1.0 + sc_ref[0]) + sh_ref[0]).astype(BF16)

    o_ref[...] = _dot(h_scr[...], w_ref[...])


def _in_proj(xf, g, sh, sc, w_bf, seq):
    m, d = xf.shape
    n = w_bf.shape[1]
    tm, tn = min(1024, seq), 512
    per_b = seq // tm
    return pl.pallas_call(
        _in_proj_kernel,
        grid=(m // tm, n // tn),
        in_specs=[pl.BlockSpec((tm, d), lambda i, j: (i, 0)),
                  pl.BlockSpec((1, d), lambda i, j: (0, 0)),
                  pl.BlockSpec((1, 1, d), lambda i, j: (i // per_b, 0, 0)),
                  pl.BlockSpec((1, 1, d), lambda i, j: (i // per_b, 0, 0)),
                  pl.BlockSpec((d, tn), lambda i, j: (0, j))],
        out_specs=pl.BlockSpec((tm, tn), lambda i, j: (i, j)),
        out_shape=jax.ShapeDtypeStruct((m, n), F32),
        scratch_shapes=[pltpu.VMEM((tm, d), BF16)],
        compiler_params=_cparams(("arbitrary", "arbitrary"), VMEM_LIMIT),
        name="in_proj",
    )(xf, g.reshape(1, d), sh, sc, w_bf)


def _gmlp_kernel(u_ref, v_ref, vg_ref, vb_ref, ws_ref, bs_ref, og_ref, o_ref):
    gu = _gelu(u_ref[...])
    gv = _gelu(v_ref[...])
    mu = jnp.mean(gv, axis=-1, keepdims=True)
    var = jnp.mean(jnp.square(gv - mu), axis=-1, keepdims=True)
    vn = ((gv - mu) * lax.rsqrt(var + LN_EPS) * vg_ref[...] + vb_ref[...]).astype(BF16)
    ri = lax.broadcasted_iota(I32, (GMLP_BLOCK, GMLP_BLOCK), 0) // CHUNK
    ci = lax.broadcasted_iota(I32, (GMLP_BLOCK, GMLP_BLOCK), 1) // CHUNK
    keep = ci <= ri
    bs = bs_ref[...]
    parts = []
    for h in range(GMLP_HEADS):
        w = jnp.where(keep, ws_ref[h], 0.0).astype(BF16)
        z = _dot(w, vn[:, h * LANES:(h + 1) * LANES]) + bs[:, h:h + 1]
        parts.append(gu[:, h * LANES:(h + 1) * LANES] * z)
    y = jnp.concatenate(parts, axis=-1)
    o_ref[...] = y * lax.rsqrt(jnp.mean(y * y, axis=-1, keepdims=True) + NORM_EPS) * og_ref[...]


def _gmlp(p, vg, vb, ws, bs, og):
    m = p.shape[0]
    w = GMLP_WIDTH
    return pl.pallas_call(
        _gmlp_kernel,
        grid=(m // GMLP_BLOCK,),
        in_specs=[pl.BlockSpec((GMLP_BLOCK, w), lambda i: (i, 0)),
                  pl.BlockSpec((GMLP_BLOCK, w), lambda i: (i, 1)),
                  pl.BlockSpec((1, w), lambda i: (0, 0)),
                  pl.BlockSpec((1, w), lambda i: (0, 0)),
                  pl.BlockSpec((GMLP_HEADS, GMLP_BLOCK, GMLP_BLOCK), lambda i: (0, 0, 0)),
                  pl.BlockSpec((GMLP_BLOCK, GMLP_HEADS), lambda i: (0, 0)),
                  pl.BlockSpec((1, w), lambda i: (0, 0))],
        out_specs=pl.BlockSpec((GMLP_BLOCK, w), lambda i: (i, 0)),
        out_shape=jax.ShapeDtypeStruct((m, w), F32),
        compiler_params=_cparams(("arbitrary",), VMEM_LIMIT),
        name="gmlp",
    )(p, p, vg.reshape(1, w), vb.reshape(1, w), ws, bs.T, og.reshape(1, w))


def _shift_mix(p, prev_row, mu):
    rolled = pltpu.roll(p, 1, 0)
    first = lax.broadcasted_iota(I32, p.shape, 0) == 0
    shifted = jnp.where(first, prev_row, rolled)
    return p + (shifted - p) * mu


def _rwkv_prep_kernel(r_ref, k_ref, v_ref, l_ref, mur_ref, muk_ref, muv_ref, mul_ref,
                      w0_ref, w2_ref, a0_ref, a2_ref, g2_ref, kkw_ref, ka_ref, hsum_ref,
                      ro_ref, ko_ref, vo_ref, kko_ref, bo_ref, lw_ref, go_ref,
                      pr_scr, pk_scr, pv_scr, plr_scr):
    @pl.when(pl.program_id(1) == 0)
    def _():
        pr_scr[...] = jnp.zeros_like(pr_scr)
        pk_scr[...] = jnp.zeros_like(pk_scr)
        pv_scr[...] = jnp.zeros_like(pv_scr)
        plr_scr[...] = jnp.zeros_like(plr_scr)

    rows = r_ref.shape[0]
    r_raw, k_raw, v_raw, l_raw = r_ref[...], k_ref[...], v_ref[...], l_ref[...]
    r = _shift_mix(r_raw, pr_scr[...], mur_ref[...])
    k = _shift_mix(k_raw, pk_scr[...], muk_ref[...])
    v = _shift_mix(v_raw, pv_scr[...], muv_ref[...])
    lo = _shift_mix(l_raw, plr_scr[...], mul_ref[...])
    pr_scr[...] = r_raw[rows - 1:rows, :]
    pk_scr[...] = k_raw[rows - 1:rows, :]
    pv_scr[...] = v_raw[rows - 1:rows, :]
    plr_scr[...] = l_raw[rows - 1:rows, :]

    wpre = w0_ref[...] + _dot(jnp.tanh(lo).astype(BF16), w2_ref[...])
    w = -jax.nn.softplus(-wpre) - 0.5
    a = jax.nn.sigmoid(a0_ref[...] + _dot(lo.astype(BF16), a2_ref[...]))
    g = _dot(jax.nn.sigmoid(lo).astype(BF16), g2_ref[...])

    kks = k * kkw_ref[...]
    sq = kks * kks
    hsum = hsum_ref[...]
    ss = jnp.concatenate([_dot(sq[:, j * LANES:(j + 1) * LANES], hsum, precision=HI)
                          for j in range(RWKV_WIDTH // LANES)], axis=-1)
    kk = kks / jnp.maximum(jnp.sqrt(ss), 1e-12)
    ro_ref[...] = r
    ko_ref[...] = k * (1.0 + (a - 1.0) * ka_ref[...])
    vo_ref[...] = v
    kko_ref[...] = kk
    bo_ref[...] = kk * a
    lw_ref[...] = -jnp.exp(w)
    go_ref[...] = g


def _rwkv_prep(p, mu, w0, w2, a0, a2, g2, kkw, ka, seq):
    m = p.shape[0]
    w = RWKV_WIDTH
    ts = min(256, seq)
    nb = seq // ts
    lora_cols = DECAY_LORA + AAA_LORA + GATE_LORA
    mu_r, mu_k, mu_v = mu[0:w], mu[w:2 * w], mu[2 * w:3 * w]
    mu_l = jnp.pad(mu[3 * w:], (0, LORA_PAD - lora_cols))
    lane_head = jnp.arange(LANES) // HEAD_DIM
    hsum = (lane_head[:, None] == lane_head[None, :]).astype(F32)
    pad_rows = lambda t, at: jnp.pad(t, ((at, LORA_PAD - at - t.shape[0]), (0, 0))).astype(BF16)
    w2 = pad_rows(w2, 0)
    a2 = pad_rows(a2, DECAY_LORA)
    g2 = pad_rows(g2, DECAY_LORA + AAA_LORA)
    row = lambda a_: a_.reshape(1, -1)
    big = pl.BlockSpec((ts, w), lambda b, i: (b * nb + i, 0))
    const = lambda shape: pl.BlockSpec(shape, lambda b, i: (0,) * len(shape))
    outs = pl.pallas_call(
        _rwkv_prep_kernel,
        grid=(m // seq, nb),
        in_specs=[pl.BlockSpec((ts, w), lambda b, i: (b * nb + i, 2)),
                  pl.BlockSpec((ts, w), lambda b, i: (b * nb + i, 3)),
                  pl.BlockSpec((ts, w), lambda b, i: (b * nb + i, 4)),
                  pl.BlockSpec((ts, LORA_PAD), lambda b, i: (b * nb + i, 10)),
                  const((1, w)), const((1, w)), const((1, w)), const((1, LORA_PAD)),
                  const((1, w)), const((LORA_PAD, w)), const((1, w)), const((LORA_PAD, w)),
                  const((LORA_PAD, w)), const((1, w)), const((1, w)), const((LANES, LANES))],
        out_specs=[big] * 7,
        out_shape=[jax.ShapeDtypeStruct((m, w), F32)] * 7,
        scratch_shapes=[pltpu.VMEM((1, w), F32), pltpu.VMEM((1, w), F32), pltpu.VMEM((1, w), F32),
                        pltpu.VMEM((1, LORA_PAD), F32)],
        compiler_params=_cparams(("arbitrary", "arbitrary"), VMEM_LIMIT),
        name="rwkv_prep",
    )(p, p, p, p, row(mu_r), row(mu_k), row(mu_v), row(mu_l),
      row(w0), w2, row(a0), a2, g2, row(kkw), row(ka), hsum)
    return outs


def _stack_heads(x):
    lane = lax.broadcasted_iota(I32, x.shape, 1)
    return jnp.concatenate([jnp.where(lane < HEAD_DIM, x, 0.0), jnp.where(lane >= HEAD_DIM, x, 0.0)], axis=0)


def _unit_lower_inverse(a):
    n = a.shape[0]
    ri = lax.broadcasted_iota(I32, (n, n), 0)
    ci = lax.broadcasted_iota(I32, (n, n), 1)
    eye = (ri == ci).astype(F32)
    diag_blk = (ri // 16) == (ci // 16)
    d = jnp.where(diag_blk, a, 0.0)
    low = a - d
    mm = functools.partial(_dot, precision=HI)
    d2 = mm(d, d)
    d4 = mm(d2, d2)
    d8 = mm(d4, d4)
    td = eye + d
    td = td + mm(td, d2)
    td = td + mm(td, d4)
    td = td + mm(td, d8)
    nn = mm(td, low)
    n2 = mm(nn, nn)
    x = eye + nn
    x = x + mm(x, n2)
    return mm(x, td)


def _rwkv_scan_kernel(r_ref, k_ref, v_ref, kk_ref, b_ref, lw_ref, g_ref, rk_ref, lng_ref, lnb_ref,
                      o_ref, s_scr):
    c = SCAN_CHUNK

    @pl.when(pl.program_id(2) == 0)
    def _():
        s_scr[...] = jnp.zeros_like(s_scr)

    r, k, v, kk, bb, lw = r_ref[...], k_ref[...], v_ref[...], kk_ref[...], b_ref[...], lw_ref[...]
    ti = lax.broadcasted_iota(I32, (c, c), 0)
    si = lax.broadcasted_iota(I32, (c, c), 1)
    tril = (si <= ti).astype(F32)
    cum = _dot(tril, lw, precision=HI)
    wcur = jnp.exp(cum)
    winv = jnp.exp(-cum)
    wprev = jnp.exp(cum - lw)
    mmh = functools.partial(_dot, precision=HI)
    rt = _stack_heads(r * wcur)
    kt = _stack_heads(k * winv)
    bt = _stack_heads(bb * winv)
    at = _stack_heads(-kk * wprev)
    vs = _stack_heads(v)

    n = 2 * c
    ri = lax.broadcasted_iota(I32, (n, n), 0)
    ci = lax.broadcasted_iota(I32, (n, n), 1)
    same = (ri // c) == (ci // c)
    strict = same & (ci < ri)
    incl = same & (ci <= ri)
    a_ab = jnp.where(strict, _dot_nt(at, bt, precision=HI), 0.0)
    a_ak = jnp.where(strict, _dot_nt(at, kt, precision=HI), 0.0)
    a_rb = jnp.where(incl, _dot_nt(rt, bt, precision=HI), 0.0)
    a_rk = jnp.where(incl, _dot_nt(rt, kt, precision=HI), 0.0)

    tinv = _unit_lower_inverse(a_ab)
    av = mmh(a_ak, vs)
    pm = mmh(tinv, at)
    qm = mmh(tinv, av)
    r2 = rt + mmh(a_rb, pm)
    y0 = mmh(a_rb, qm) + mmh(a_rk, vs)
    wc = wcur[c - 1:c, :]
    eye = (ri == ci).astype(F32)
    m_k = (eye + _dot_tn(pm, bt, precision=HI)) * wc
    n_k = (_dot_tn(qm, bt, precision=HI) + _dot_tn(vs, kt, precision=HI)) * wc
    s0 = s_scr[...]
    y = _dot_nt(r2, s0, precision=HI) + y0
    s_scr[...] = mmh(s0, m_k) + n_k

    yp = y[0:c, :] + y[c:n, :]
    lane = lax.broadcasted_iota(I32, (LANES, LANES), 0) // HEAD_DIM
    lane_t = lax.broadcasted_iota(I32, (LANES, LANES), 1) // HEAD_DIM
    havg = jnp.where(lane == lane_t, 1.0 / HEAD_DIM, 0.0).astype(F32)
    mu = mmh(yp, havg)
    dev = yp - mu
    var = mmh(dev * dev, havg)
    yn = dev * lax.rsqrt(var + GROUPNORM_EPS) * lng_ref[...] + lnb_ref[...]
    bonus = mmh(r * k * rk_ref[...], havg) * HEAD_DIM * v
    o_ref[...] = (yn + bonus) * g_ref[...]


def _rwkv_scan(r, k, v, kk, bvec, lw, g, rk, lng, lnb, seq):
    m = r.shape[0]
    c = SCAN_CHUNK
    nc = seq // c
    npair = RWKV_WIDTH // LANES
    blk = pl.BlockSpec((c, LANES), lambda b, h, i: (b * nc + i, h))
    prm = pl.BlockSpec((1, LANES), lambda b, h, i: (0, h))
    return pl.pallas_call(
        _rwkv_scan_kernel,
        grid=(m // seq, npair, nc),
        in_specs=[blk] * 7 + [prm] * 3,
        out_specs=blk,
        out_shape=jax.ShapeDtypeStruct((m, RWKV_WIDTH), F32),
        scratch_shapes=[pltpu.VMEM((LANES, LANES), F32)],
        compiler_params=_cparams(("arbitrary", "arbitrary", "arbitrary"), VMEM_LIMIT),
        name="rwkv_scan",
    )(r, k, v, kk, bvec, lw, g, rk.reshape(1, -1), lng.reshape(1, -1), lnb.reshape(1, -1))


def _out_proj_kernel(x_ref, ya_ref, yb_ref, wa_ref, wb_ref, gt_ref, g2_ref, sh_ref, sc_ref, x1_ref, h_ref):
    y = _dot(ya_ref[...].astype(BF16), wa_ref[...]) + _dot(yb_ref[...].astype(BF16), wb_ref[...])
    x1 = x_ref[...] + gt_ref[0] * y
    x1_ref[...] = x1
    hn = x1 * lax.rsqrt(jnp.mean(x1 * x1, axis=-1, keepdims=True) + NORM_EPS) * g2_ref[...]
    h_ref[...] = hn * (1.0 + sc_ref[0]) + sh_ref[0]


def _out_proj(xf, ya, yb, w_bf, gt, g2, sh, sc, seq):
    m, d = xf.shape
    tm = min(256, seq)
    per_b = seq // tm
    half = GMLP_WIDTH
    bspec = pl.BlockSpec((1, 1, d), lambda i: (i // per_b, 0, 0))
    return pl.pallas_call(
        _out_proj_kernel,
        grid=(m // tm,),
        in_specs=[pl.BlockSpec((tm, d), lambda i: (i, 0)),
                  pl.BlockSpec((tm, half), lambda i: (i, 0)),
                  pl.BlockSpec((tm, half), lambda i: (i, 0)),
                  pl.BlockSpec((half, d), lambda i: (0, 0)),
                  pl.BlockSpec((half, d), lambda i: (1, 0)),
                  bspec,
                  pl.BlockSpec((1, d), lambda i: (0, 0)),
                  bspec, bspec],
        out_specs=[pl.BlockSpec((tm, d), lambda i: (i, 0)), pl.BlockSpec((tm, d), lambda i: (i, 0))],
        out_shape=[jax.ShapeDtypeStruct((m, d), F32), jax.ShapeDtypeStruct((m, d), F32)],
        compiler_params=_cparams(("arbitrary",), VMEM_LIMIT),
        name="out_proj",
    )(xf, ya, yb, w_bf, w_bf, gt, g2.reshape(1, d), sh, sc)


def _q_scores_kernel(h_ref, wq_ref, keys_ref, o_ref):
    q = _dot(h_ref[...].astype(BF16), wq_ref[...]).astype(BF16)
    for hp in range(2 * PEER_HEADS):
        o_ref[hp * PEER_KEYS:(hp + 1) * PEER_KEYS, :] = _dot_nt(
            keys_ref[hp], q[:, hp * LANES:(hp + 1) * LANES])


def _q_scores(h, wq_bf, keys_bf):
    m, d = h.shape
    tm = 256
    n = wq_bf.shape[1]
    return pl.pallas_call(
        _q_scores_kernel,
        grid=(m // tm,),
        in_specs=[pl.BlockSpec((tm, d), lambda i: (i, 0)),
                  pl.BlockSpec((d, n), lambda i: (0, 0)),
                  pl.BlockSpec((2 * PEER_HEADS, PEER_KEYS, LANES), lambda i: (0, 0, 0))],
        out_specs=pl.BlockSpec((n, tm), lambda i: (0, i)),
        out_shape=jax.ShapeDtypeStruct((n, m), F32),
        compiler_params=_cparams(("arbitrary",), VMEM_LIMIT),
        name="q_scores",
    )(h, wq_bf, keys_bf)


def _extract_topk(vals, k):
    rows = vals.shape[0]
    iota = lax.broadcasted_iota(I32, vals.shape, 0)
    tops, poss = [], []
    for _ in range(k):
        m = jnp.max(vals, axis=0)
        pos = jnp.min(jnp.where(vals == m[None], iota, rows), axis=0)
        vals = jnp.where(iota == pos[None], -jnp.inf, vals)
        tops.append(m)
        poss.append(pos)
    return tops, poss


def _topk_kernel(s_ref, e_ref, g_ref, st_scr, it_scr):
    k = PEER_TOPK

    def half_body(hp, carry):
        vals = s_ref[pl.ds(pl.multiple_of(hp * PEER_KEYS, PEER_KEYS), PEER_KEYS)]
        tops, poss = _extract_topk(vals, k)
        for j in range(k):
            st_scr[hp, j] = tops[j]
            it_scr[hp, j] = poss[j]
        return carry

    lax.fori_loop(0, 2 * PEER_HEADS, half_body, 0)

    def head_body(h, carry):
        s1 = [st_scr[2 * h, j] for j in range(k)]
        s2 = [st_scr[2 * h + 1, j] for j in range(k)]
        i1 = [it_scr[2 * h, j] for j in range(k)]
        i2 = [it_scr[2 * h + 1, j] for j in range(k)]
        cand = jnp.stack([s1[a] + s2[b] for a in range(k) for b in range(k)], axis=0)
        tops, poss = _extract_topk(cand, k)
        exps = [jnp.exp(t - tops[0]) for t in tops]
        den = exps[0]
        for t in exps[1:]:
            den = den + t
        for j in range(k):
            pa = poss[j] >> 4
            pb = poss[j] & (k - 1)
            ia = jnp.zeros_like(pa)
            ib = jnp.zeros_like(pa)
            for a in range(k):
                ia = jnp.where(pa == a, i1[a], ia)
                ib = jnp.where(pb == a, i2[a], ib)
            e_ref[h * k + j] = ia * PEER_KEYS + ib
            g_ref[h * k + j] = exps[j] / den
        return carry

    lax.fori_loop(0, PEER_HEADS, head_body, 0)


def _topk(scores3):
    n, nb, _ = scores3.shape
    blk = pl.BlockSpec((N_SEL, SUBLANES, LANES), lambda i: (0, i, 0))
    return pl.pallas_call(
        _topk_kernel,
        grid=(nb // SUBLANES,),
        in_specs=[pl.BlockSpec((n, SUBLANES, LANES), lambda i: (0, i, 0))],
        out_specs=[blk, blk],
        out_shape=[jax.ShapeDtypeStruct((N_SEL, nb, LANES), I32), jax.ShapeDtypeStruct((N_SEL, nb, LANES), F32)],
        scratch_shapes=[pltpu.VMEM((2 * PEER_HEADS, PEER_TOPK, SUBLANES, LANES), F32),
                        pltpu.VMEM((2 * PEER_HEADS, PEER_TOPK, SUBLANES, LANES), I32)],
        compiler_params=_cparams(("arbitrary",), VMEM_LIMIT),
        name="topk",
    )(scores3)


def _route_kernel(e_ref, g_ref, ids_ref, gs_ref, st_ref, q_scr):
    zero = jnp.zeros((SUBLANES, LANES), I32)
    counts = [zero] * N_STRIPS
    for kk in range(N_SEL):
        strip = e_ref[kk] >> STRIP_SHIFT
        rank = zero
        for s in range(N_STRIPS):
            hit = strip == s
            rank = jnp.where(hit, counts[s], rank)
            counts[s] = counts[s] + hit.astype(I32)
        q_scr[kk] = rank
    start = zero
    starts = []
    for s in range(N_STRIPS):
        starts.append(start)
        st_ref[s] = start
        start = start + counts[s]
    st_ref[N_STRIPS] = start

    def fix(kk, carry):
        e = e_ref[kk]
        strip = e >> STRIP_SHIFT
        base = zero
        for s in range(N_STRIPS):
            base = jnp.where(strip == s, starts[s], base)
        q_scr[kk] = q_scr[kk] + base
        return carry

    lax.fori_loop(0, N_SEL, fix, 0)

    def place(q, carry):
        acc_i = zero
        acc_g = jnp.zeros((SUBLANES, LANES), F32)
        for kk in range(N_SEL):
            hit = q_scr[kk] == q
            acc_i = jnp.where(hit, e_ref[kk], acc_i)
            acc_g = jnp.where(hit, g_ref[kk], acc_g)
        ids_ref[q] = acc_i & (STRIP - 1)
        gs_ref[q] = acc_g
        return carry

    lax.fori_loop(0, N_SEL, place, 0)


def _route(e3, g3):
    _, nb, _ = e3.shape
    blk = pl.BlockSpec((N_SEL, SUBLANES, LANES), lambda i: (0, i, 0))
    sblk = pl.BlockSpec((N_STRIPS + 1, SUBLANES, LANES), lambda i: (0, i, 0))
    return pl.pallas_call(
        _route_kernel,
        grid=(nb // SUBLANES,),
        in_specs=[blk, blk],
        out_specs=[blk, blk, sblk],
        out_shape=[jax.ShapeDtypeStruct((N_SEL, nb, LANES), I32), jax.ShapeDtypeStruct((N_SEL, nb, LANES), F32),
                   jax.ShapeDtypeStruct((N_STRIPS + 1, nb, LANES), I32)],
        scratch_shapes=[pltpu.VMEM((N_SEL, SUBLANES, LANES), I32)],
        compiler_params=_cparams(("arbitrary",), VMEM_LIMIT),
        name="route",
    )(e3, g3)


def _pack_kernel(t_ref, o_ref):
    x = t_ref[...]
    half = x.shape[1] // 2
    lo = lax.bitcast_convert_type(x[:, :half].astype(BF16).astype(F32), I32)
    hi = lax.bitcast_convert_type(x[:, half:].astype(BF16).astype(F32), I32)
    o_ref[...] = (hi & jnp.int32(-65536)) | lax.shift_right_logical(lo, jnp.int32(16))


def _pack(table):
    n, d = table.shape
    tr = 512
    return pl.pallas_call(
        _pack_kernel,
        grid=(n // tr,),
        in_specs=[pl.BlockSpec((tr, d), lambda i: (i, 0))],
        out_specs=pl.BlockSpec((tr, d // 2), lambda i: (i, 0)),
        out_shape=jax.ShapeDtypeStruct((n, d // 2), I32),
        compiler_params=_cparams(("arbitrary",), VMEM_LIMIT),
        name="pack",
    )(table)


def _unpack(word):
    lo = lax.bitcast_convert_type(word << 16, F32)
    hi = lax.bitcast_convert_type(word & jnp.int32(-65536), F32)
    return lo, hi


def _peer_kernel(ids_ref, gs_ref, st_ref, h_ref, u_ref, v_ref, o_ref, r_scr, z_scr):
    s = pl.program_id(0)
    nslot = SUB_TOK * ROUND_SLOTS
    ones = jnp.ones((LANES, LANES), F32)

    def sub_body(sub, carry):
        tok0 = sub * SUB_TOK
        begs, cnts = [], []
        cmax = jnp.int32(0)
        for t in range(SUB_TOK):
            b0 = st_ref[0, 0, (tok0 + t) * (N_STRIPS + 1) + s]
            b1 = st_ref[0, 0, (tok0 + t) * (N_STRIPS + 1) + s + 1]
            begs.append(b0)
            cnts.append(b1 - b0)
            cmax = jnp.maximum(cmax, b1 - b0)
        for t in range(SUB_TOK):
            o_ref[tok0 + t] = jnp.zeros((2, SUBLANES, LANES), F32)
        nrounds = (cmax + ROUND_SLOTS - 1) // ROUND_SLOTS

        def round_body(rd, carry2):
            rows, gates = [], []
            for t in range(SUB_TOK):
                h0 = h_ref[tok0 + t, 0]
                h1 = h_ref[tok0 + t, 1]
                for j in range(ROUND_SLOTS):
                    pos = rd * ROUND_SLOTS + j
                    ok = pos < cnts[t]
                    src = (tok0 + t) * N_SEL + jnp.where(ok, begs[t] + pos, 0)
                    row = jnp.where(ok, ids_ref[0, 0, src], 0)
                    gate = jnp.where(ok, gs_ref[0, 0, src], 0.0)
                    rows.append(row)
                    gates.append(gate)
                    lo, hi = _unpack(u_ref[0, row])
                    part = lo * h0 + hi * h1
                    r_scr[pl.ds(t * ROUND_SLOTS + j, 1), :] = jnp.sum(part, axis=0, keepdims=True)
            z = _dot(r_scr[...], ones, precision=HI)
            z_scr[...] = _gelu(z)
            for t in range(SUB_TOK):
                acc0 = jnp.zeros((SUBLANES, LANES), F32)
                acc1 = jnp.zeros((SUBLANES, LANES), F32)
                for j in range(ROUND_SLOTS):
                    i = t * ROUND_SLOTS + j
                    a = z_scr[pl.ds(i, 1), :] * gates[i]
                    lo, hi = _unpack(v_ref[0, rows[i]])
                    acc0 = acc0 + a * lo
                    acc1 = acc1 + a * hi
                o_ref[tok0 + t, 0] = o_ref[tok0 + t, 0] + acc0
                o_ref[tok0 + t, 1] = o_ref[tok0 + t, 1] + acc1
            return carry2

        lax.fori_loop(0, nrounds, round_body, 0)
        return carry

    lax.fori_loop(0, TOK_TILE // SUB_TOK, sub_body, 0)


def _peer(ids_t, gs_t, st_t, h4, u_pk, v_pk):
    ntile = ids_t.shape[0]
    m = h4.shape[0]
    smem = lambda width: pl.BlockSpec((1, 1, width), lambda s, j: (j, 0, 0), memory_space=pltpu.SMEM)
    tab = pl.BlockSpec((1, STRIP, SUBLANES, LANES), lambda s, j: (s, 0, 0, 0))
    return pl.pallas_call(
        _peer_kernel,
        grid=(N_STRIPS, ntile),
        in_specs=[smem(TOK_TILE * N_SEL), smem(TOK_TILE * N_SEL), smem(TOK_TILE * (N_STRIPS + 1)),
                  pl.BlockSpec((TOK_TILE, 2, SUBLANES, LANES), lambda s, j: (j, 0, 0, 0)),
                  tab, tab],
        out_specs=pl.BlockSpec((None, TOK_TILE, 2, SUBLANES, LANES), lambda s, j: (s, j, 0, 0, 0)),
        out_shape=jax.ShapeDtypeStruct((N_STRIPS, m, 2, SUBLANES, LANES), F32),
        scratch_shapes=[pltpu.VMEM((SUB_TOK * ROUND_SLOTS, LANES), F32),
                        pltpu.VMEM((SUB_TOK * ROUND_SLOTS, LANES), F32)],
        compiler_params=_cparams(("arbitrary", "arbitrary"), VMEM_LIMIT),
        name="peer",
    )(ids_t, gs_t, st_t, h4, u_pk, v_pk)


def _final_kernel(x_ref, p_ref, gt_ref, g_ref, o_ref):
    acc = p_ref[0]
    for s in range(1, N_STRIPS):
        acc = acc + p_ref[s]
    x2 = x_ref[...] + gt_ref[0] * acc
    o_ref[...] = x2 * lax.rsqrt(jnp.mean(x2 * x2, axis=-1, keepdims=True) + NORM_EPS) * g_ref[...]


def _final(x1, parts, gt, g, seq):
    m, d = x1.shape
    tm = min(128, seq)
    per_b = seq // tm
    return pl.pallas_call(
        _final_kernel,
        grid=(m // tm,),
        in_specs=[pl.BlockSpec((tm, d), lambda i: (i, 0)),
                  pl.BlockSpec((N_STRIPS, tm, d), lambda i: (0, i, 0)),
                  pl.BlockSpec((1, 1, d), lambda i: (i // per_b, 0, 0)),
                  pl.BlockSpec((1, d), lambda i: (0, 0))],
        out_specs=pl.BlockSpec((tm, d), lambda i: (i, 0)),
        out_shape=jax.ShapeDtypeStruct((m, d), F32),
        compiler_params=_cparams(("arbitrary",), VMEM_LIMIT),
        name="final",
    )(x1, parts, gt, g.reshape(1, d))


def _token_mixing(xf, seq, sh1, sc1, gt1, norm1_g, w_in, gmlp_v_g, gmlp_v_b, gmlp_ws, gmlp_bs, gmlp_out_g,
                  rwkv_mu, rwkv_w0, rwkv_w2, rwkv_a0, rwkv_a2, rwkv_g2, rwkv_kk, rwkv_ka, rwkv_rk,
                  rwkv_ln_g, rwkv_ln_b):
    w_pad = jnp.pad(w_in, ((0, 0), (0, IN_PAD - IN_WIDTH))).astype(BF16)
    p = _in_proj(xf, norm1_g, sh1, sc1, w_pad, seq)
    y_a = _gmlp(p, gmlp_v_g, gmlp_v_b, gmlp_ws, gmlp_bs, gmlp_out_g)
    r, k, v, kk, bvec, lw, g = _rwkv_prep(p, rwkv_mu, rwkv_w0, rwkv_w2, rwkv_a0, rwkv_a2, rwkv_g2,
                                          rwkv_kk, rwkv_ka, seq)
    y_b = _rwkv_scan(r, k, v, kk, bvec, lw, g, rwkv_rk, rwkv_ln_g, rwkv_ln_b, seq)
    return y_a, y_b


def _peer_ffn(h2, peer_wq, peer_keys, peer_u, peer_v):
    m, d = h2.shape
    keys_bf = peer_keys.reshape(2 * PEER_HEADS, PEER_KEYS, LANES).astype(BF16)
    scores = _q_scores(h2, peer_wq.astype(BF16), keys_bf)
    e3, g3 = _topk(scores.reshape(scores.shape[0], m // LANES, LANES))
    ids3, gs3, st3 = _route(e3, g3)
    ntile = m // TOK_TILE
    tok_major = lambda a_: a_.reshape(a_.shape[0], m).T.reshape(ntile, 1, TOK_TILE * a_.shape[0])
    u_pk = _pack(peer_u).reshape(N_STRIPS, STRIP, SUBLANES, LANES)
    v_pk = _pack(peer_v).reshape(N_STRIPS, STRIP, SUBLANES, LANES)
    h4 = h2.reshape(m, 2, SUBLANES, LANES)
    parts = _peer(tok_major(ids3), tok_major(gs3), tok_major(st3), h4, u_pk, v_pk)
    return parts.reshape(N_STRIPS, m, d)


def kernel(x, c, ada_w, ada_b, norm1_g, w_in, gmlp_v_g, gmlp_v_b, gmlp_ws, gmlp_bs, gmlp_out_g, rwkv_mu, rwkv_w0, rwkv_w2, rwkv_a0, rwkv_a2, rwkv_g2, rwkv_kk, rwkv_ka, rwkv_rk, rwkv_ln_g, rwkv_ln_b, w_out, norm2_g, peer_wq, peer_keys, peer_u, peer_v, final_g):
    bsz, seq, d = x.shape
    depth = ada_w.shape[0]
    xf = x.reshape(bsz * seq, d)
    for l in range(depth):
        mod = _mod(c, ada_w[l], ada_b[l])
        sh1, sc1, gt1, sh2, sc2, gt2 = [t.reshape(bsz, 1, d) for t in jnp.split(mod, 6, axis=-1)]
        y_a, y_b = _token_mixing(xf, seq, sh1, sc1, gt1, norm1_g[l], w_in[l], gmlp_v_g[l], gmlp_v_b[l],
                                 gmlp_ws[l], gmlp_bs[l], gmlp_out_g[l], rwkv_mu[l], rwkv_w0[l], rwkv_w2[l],
                                 rwkv_a0[l], rwkv_a2[l], rwkv_g2[l], rwkv_kk[l], rwkv_ka[l], rwkv_rk[l],
                                 rwkv_ln_g[l], rwkv_ln_b[l])
        x1, h2 = _out_proj(xf, y_a, y_b, w_out[l].astype(BF16), gt1, norm2_g[l], sh2, sc2, seq)
        parts = _peer_ffn(h2, peer_wq[l], peer_keys[l], peer_u[l], peer_v[l])
        last = l == depth - 1
        assert last, "multi-layer stacking needs an un-normalised residual output"
        xf = _final(x1, parts, gt2, final_g, seq)
    return xf.reshape(bsz, seq, d)
```

```python
import functools

import jax
import jax.numpy as jnp
from jax import lax
from jax.experimental import pallas as pl
from jax.experimental.pallas import tpu as pltpu

F32 = jnp.float32
BF16 = jnp.bfloat16
I32 = jnp.int32
HI = lax.Precision.HIGHEST

D_MODEL = 2048
GMLP_WIDTH = 1024
RWKV_WIDTH = 1024
GMLP_BLOCK = 128
GMLP_HEADS = 8
CHUNK = 64
HEAD_DIM = 64
DECAY_LORA = 64
AAA_LORA = 64
GATE_LORA = 160
LORA_PAD = 512
IN_WIDTH = 5408
IN_PAD = 5632
PEER_KEYS = 128
PEER_HEADS = 8
PEER_TOPK = 16
N_SEL = PEER_HEADS * PEER_TOPK
N_EXPERTS = PEER_KEYS * PEER_KEYS
N_STRIPS = 8
STRIP = N_EXPERTS // N_STRIPS
STRIP_SHIFT = STRIP.bit_length() - 1
TABLE_ROWS = 16
NORM_EPS = 1e-6
LN_EPS = 1e-5
GROUPNORM_EPS = 64e-5
SCAN_CHUNK = 64
LANES = 128
SUBLANES = 8
TOK_TILE = 64
SUB_TOK = 8
ROUND_SLOTS = 8
IDS_PAD = 384
VMEM_LIMIT = 56 * 1024 * 1024


def _cparams(sem, vmem=None):
    return pltpu.CompilerParams(dimension_semantics=sem, vmem_limit_bytes=vmem)


def _gelu(x):
    return 0.5 * x * (1.0 + lax.erf(x * 0.7071067811865476))


def _dot(a, b, **kw):
    return jnp.dot(a, b, preferred_element_type=F32, **kw)


def _dot_nt(a, b, **kw):
    return lax.dot_general(a, b, (((1,), (1,)), ((), ())), preferred_element_type=F32, **kw)


def _dot_tn(a, b, **kw):
    return lax.dot_general(a, b, (((0,), (0,)), ((), ())), preferred_element_type=F32, **kw)


def _mod_kernel(c_ref, w_ref, b_ref, o_ref):
    c = c_ref[...]
    sc = (c * jax.nn.sigmoid(c)).astype(BF16)
    o_ref[...] = _dot(sc, w_ref[...].astype(BF16)) + b_ref[...]


def _mod(c, w, b):
    bsz, d = c.shape
    n = w.shape[1]
    tn = 1024
    return pl.pallas_call(
        _mod_kernel,
        grid=(n // tn,),
        in_specs=[pl.BlockSpec((bsz, d), lambda j: (0, 0)),
                  pl.BlockSpec((d, tn), lambda j: (0, j)),
                  pl.BlockSpec((1, tn), lambda j: (0, j))],
        out_specs=pl.BlockSpec((bsz, tn), lambda j: (0, j)),
        out_shape=jax.ShapeDtypeStruct((bsz, n), F32),
        compiler_params=_cparams(("arbitrary",), VMEM_LIMIT),
        name="mod",
    )(c, w, b.reshape(1, n))


def _in_proj_kernel(x_ref, g_ref, sh_ref, sc_ref, w_ref, o_ref, h_scr):
    @pl.when(pl.program_id(1) == 0)
    def _():
        x = x_ref[...]
        y = x * lax.rsqrt(jnp.mean(x * x, axis=-1, keepdims=True) + NORM_EPS) * g_ref[...]
        h_scr[...] = (y * (1.0 + sc_ref[0]) + sh_ref[0]).astype(BF16)

    o_ref[...] = _dot(h_scr[...], w_ref[...])


def _in_proj(xf, g, sh, sc, w_bf, seq):
    m, d = xf.shape
    n = w_bf.shape[1]
    tm, tn = min(1024, seq), 512
    per_b = seq // tm
    return pl.pallas_call(
        _in_proj_kernel,
        grid=(m // tm, n // tn),
        in_specs=[pl.BlockSpec((tm, d), lambda i, j: (i, 0)),
                  pl.BlockSpec((1, d), lambda i, j: (0, 0)),
                  pl.BlockSpec((1, 1, d), lambda i, j: (i // per_b, 0, 0)),
                  pl.BlockSpec((1, 1, d), lambda i, j: (i // per_b, 0, 0)),
                  pl.BlockSpec((d, tn), lambda i, j: (0, j))],
        out_specs=pl.BlockSpec((tm, tn), lambda i, j: (i, j)),
        out_shape=jax.ShapeDtypeStruct((m, n), F32),
        scratch_shapes=[pltpu.VMEM((tm, d), BF16)],
        compiler_params=_cparams(("arbitrary", "arbitrary"), VMEM_LIMIT),
        name="in_proj",
    )(xf, g.reshape(1, d), sh, sc, w_bf)


def _gmlp_kernel(u_ref, v_ref, vg_ref, vb_ref, ws_ref, bs_ref, og_ref, o_ref):
    gu = _gelu(u_ref[...])
    gv = _gelu(v_ref[...])
    mu = jnp.mean(gv, axis=-1, keepdims=True)
    var = jnp.mean(jnp.square(gv - mu), axis=-1, keepdims=True)
    vn = ((gv - mu) * lax.rsqrt(var + LN_EPS) * vg_ref[...] + vb_ref[...]).astype(BF16)
    ri = lax.broadcasted_iota(I32, (GMLP_BLOCK, GMLP_BLOCK), 0) // CHUNK
    ci = lax.broadcasted_iota(I32, (GMLP_BLOCK, GMLP_BLOCK), 1) // CHUNK
    keep = ci <= ri
    bs = bs_ref[...]
    parts = []
    for h in range(GMLP_HEADS):
        w = jnp.where(keep, ws_ref[h], 0.0).astype(BF16)
        z = _dot(w, vn[:, h * LANES:(h + 1) * LANES]) + bs[:, h:h + 1]
        parts.append(gu[:, h * LANES:(h + 1) * LANES] * z)
    y = jnp.concatenate(parts, axis=-1)
    o_ref[...] = y * lax.rsqrt(jnp.mean(y * y, axis=-1, keepdims=True) + NORM_EPS) * og_ref[...]


def _gmlp(p, vg, vb, ws, bs, og):
    m = p.shape[0]
    w = GMLP_WIDTH
    return pl.pallas_call(
        _gmlp_kernel,
        grid=(m // GMLP_BLOCK,),
        in_specs=[pl.BlockSpec((GMLP_BLOCK, w), lambda i: (i, 0)),
                  pl.BlockSpec((GMLP_BLOCK, w), lambda i: (i, 1)),
                  pl.BlockSpec((1, w), lambda i: (0, 0)),
                  pl.BlockSpec((1, w), lambda i: (0, 0)),
                  pl.BlockSpec((GMLP_HEADS, GMLP_BLOCK, GMLP_BLOCK), lambda i: (0, 0, 0)),
                  pl.BlockSpec((GMLP_BLOCK, GMLP_HEADS), lambda i: (0, 0)),
                  pl.BlockSpec((1, w), lambda i: (0, 0))],
        out_specs=pl.BlockSpec((GMLP_BLOCK, w), lambda i: (i, 0)),
        out_shape=jax.ShapeDtypeStruct((m, w), F32),
        compiler_params=_cparams(("arbitrary",), VMEM_LIMIT),
        name="gmlp",
    )(p, p, vg.reshape(1, w), vb.reshape(1, w), ws, bs.T, og.reshape(1, w))


def _shift_mix(p, prev_row, mu):
    rolled = pltpu.roll(p, 1, 0)
    first = lax.broadcasted_iota(I32, p.shape, 0) == 0
    shifted = jnp.where(first, prev_row, rolled)
    return p + (shifted - p) * mu


def _rwkv_prep_kernel(r_ref, k_ref, v_ref, l_ref, mur_ref, muk_ref, muv_ref, mul_ref,
                      w0_ref, w2_ref, a0_ref, a2_ref, g2_ref, kkw_ref, ka_ref, hsum_ref,
                      ro_ref, ko_ref, vo_ref, kko_ref, bo_ref, lw_ref, go_ref,
                      pr_scr, pk_scr, pv_scr, plr_scr):
    @pl.when(pl.program_id(1) == 0)
    def _():
        pr_scr[...] = jnp.zeros_like(pr_scr)
        pk_scr[...] = jnp.zeros_like(pk_scr)
        pv_scr[...] = jnp.zeros_like(pv_scr)
        plr_scr[...] = jnp.zeros_like(plr_scr)

    rows = r_ref.shape[0]
    r_raw, k_raw, v_raw, l_raw = r_ref[...], k_ref[...], v_ref[...], l_ref[...]
    r = _shift_mix(r_raw, pr_scr[...], mur_ref[...])
    k = _shift_mix(k_raw, pk_scr[...], muk_ref[...])
    v = _shift_mix(v_raw, pv_scr[...], muv_ref[...])
    lo = _shift_mix(l_raw, plr_scr[...], mul_ref[...])
    pr_scr[...] = r_raw[rows - 1:rows, :]
    pk_scr[...] = k_raw[rows - 1:rows, :]
    pv_scr[...] = v_raw[rows - 1:rows, :]
    plr_scr[...] = l_raw[rows - 1:rows, :]

    wpre = w0_ref[...] + _dot(jnp.tanh(lo).astype(BF16), w2_ref[...])
    w = -jax.nn.softplus(-wpre) - 0.5
    a = jax.nn.sigmoid(a0_ref[...] + _dot(lo.astype(BF16), a2_ref[...]))
    g = _dot(jax.nn.sigmoid(lo).astype(BF16), g2_ref[...])

    kks = k * kkw_ref[...]
    sq = kks * kks
    hsum = hsum_ref[...]
    ss = jnp.concatenate([_dot(sq[:, j * LANES:(j + 1) * LANES], hsum, precision=HI)
                          for j in range(RWKV_WIDTH // LANES)], axis=-1)
    kk = kks / jnp.maximum(jnp.sqrt(ss), 1e-12)
    ro_ref[...] = r
    ko_ref[...] = k * (1.0 + (a - 1.0) * ka_ref[...])
    vo_ref[...] = v
    kko_ref[...] = kk
    bo_ref[...] = kk * a
    lw_ref[...] = -jnp.exp(w)
    go_ref[...] = g


def _rwkv_prep(p, mu, w0, w2, a0, a2, g2, kkw, ka, seq):
    m = p.shape[0]
    w = RWKV_WIDTH
    ts = min(256, seq)
    nb = seq // ts
    lora_cols = DECAY_LORA + AAA_LORA + GATE_LORA
    mu_r, mu_k, mu_v = mu[0:w], mu[w:2 * w], mu[2 * w:3 * w]
    mu_l = jnp.pad(mu[3 * w:], (0, LORA_PAD - lora_cols))
    lane_head = jnp.arange(LANES) // HEAD_DIM
    hsum = (lane_head[:, None] == lane_head[None, :]).astype(F32)
    pad_rows = lambda t, at: jnp.pad(t, ((at, LORA_PAD - at - t.shape[0]), (0, 0))).astype(BF16)
    w2 = pad_rows(w2, 0)
    a2 = pad_rows(a2, DECAY_LORA)
    g2 = pad_rows(g2, DECAY_LORA + AAA_LORA)
    row = lambda a_: a_.reshape(1, -1)
    big = pl.BlockSpec((ts, w), lambda b, i: (b * nb + i, 0))
    const = lambda shape: pl.BlockSpec(shape, lambda b, i: (0,) * len(shape))
    outs = pl.pallas_call(
        _rwkv_prep_kernel,
        grid=(m // seq, nb),
        in_specs=[pl.BlockSpec((ts, w), lambda b, i: (b * nb + i, 2)),
                  pl.BlockSpec((ts, w), lambda b, i: (b * nb + i, 3)),
                  pl.BlockSpec((ts, w), lambda b, i: (b * nb + i, 4)),
                  pl.BlockSpec((ts, LORA_PAD), lambda b, i: (b * nb + i, 10)),
                  const((1, w)), const((1, w)), const((1, w)), const((1, LORA_PAD)),
                  const((1, w)), const((LORA_PAD, w)), const((1, w)), const((LORA_PAD, w)),
                  const((LORA_PAD, w)), const((1, w)), const((1, w)), const((LANES, LANES))],
        out_specs=[big] * 7,
        out_shape=[jax.ShapeDtypeStruct((m, w), F32)] * 7,
        scratch_shapes=[pltpu.VMEM((1, w), F32), pltpu.VMEM((1, w), F32), pltpu.VMEM((1, w), F32),
                        pltpu.VMEM((1, LORA_PAD), F32)],
        compiler_params=_cparams(("arbitrary", "arbitrary"), VMEM_LIMIT),
        name="rwkv_prep",
    )(p, p, p, p, row(mu_r), row(mu_k), row(mu_v), row(mu_l),
      row(w0), w2, row(a0), a2, g2, row(kkw), row(ka), hsum)
    return outs


def _stack_heads(x):
    lane = lax.broadcasted_iota(I32, x.shape, 1)
    return jnp.concatenate([jnp.where(lane < HEAD_DIM, x, 0.0), jnp.where(lane >= HEAD_DIM, x, 0.0)], axis=0)


def _mm(a_list, b_list, dims, passes):
    dn = (dims, ((), ()))
    dg = lambda x, y, **kw: lax.dot_general(x, y, dn, preferred_element_type=F32, **kw)
    if passes == 6:
        return [dg(a, b, precision=HI) for a, b in zip(a_list, b_list)]
    ah = [a.astype(BF16) for a in a_list]
    bh = [b.astype(BF16) for b in b_list]
    main = [dg(x, y) for x, y in zip(ah, bh)]
    if passes == 1:
        return main
    al = [(a - h.astype(F32)).astype(BF16) for a, h in zip(a_list, ah)]
    bl = [(b - h.astype(F32)).astype(BF16) for b, h in zip(b_list, bh)]
    c1 = [dg(x, y) for x, y in zip(ah, bl)]
    c2 = [dg(x, y) for x, y in zip(al, bh)]
    return [m + (p + q) for m, p, q in zip(main, c1, c2)]


_NN = ((1,), (0,))
_NT = ((1,), (1,))
_TN = ((0,), (0,))
SCAN_PASSES = {"cumsum": 3, "scores": 1, "inverse": 3, "apply": 1, "state": 1, "chain": 3, "stats": 1}
SCAN_PAIRS = 4


def _unit_lower_inverse(a_list):
    n = a_list[0].shape[0]
    ri = lax.broadcasted_iota(I32, (n, n), 0)
    ci = lax.broadcasted_iota(I32, (n, n), 1)
    eye = (ri == ci).astype(F32)
    diag_blk = (ri // 16) == (ci // 16)
    mm = lambda xs, ys: _mm(xs, ys, _NN, SCAN_PASSES["inverse"])
    grow = lambda ts, ps: [t + p for t, p in zip(ts, ps)]
    d = [jnp.where(diag_blk, a, 0.0) for a in a_list]
    low = [a - x for a, x in zip(a_list, d)]
    d2 = mm(d, d)
    d4 = mm(d2, d2)
    d8 = mm(d4, d4)
    td = [eye + x for x in d]
    td = grow(td, mm(td, d2))
    td = grow(td, mm(td, d4))
    td = grow(td, mm(td, d8))
    nn = mm(td, low)
    n2 = mm(nn, nn)
    x = [eye + t for t in nn]
    x = grow(x, mm(x, n2))
    return mm(x, td)


def _scan_pairs(r, k, v, kk, bb, lw, g, rk, lng, lnb, s0):
    c = SCAN_CHUNK
    ps = SCAN_PASSES
    npair = len(r)
    ti = lax.broadcasted_iota(I32, (c, c), 0)
    si = lax.broadcasted_iota(I32, (c, c), 1)
    tril = (si <= ti).astype(F32)
    cum = _mm([tril] * npair, lw, _NN, ps["cumsum"])
    wcur = [jnp.exp(x) for x in cum]
    winv = [jnp.exp(-x) for x in cum]
    wprev = [jnp.exp(x - y) for x, y in zip(cum, lw)]
    rt = [_stack_heads(x * w) for x, w in zip(r, wcur)]
    kt = [_stack_heads(x * w) for x, w in zip(k, winv)]
    bt = [_stack_heads(x * w) for x, w in zip(bb, winv)]
    at = [_stack_heads(-x * w) for x, w in zip(kk, wprev)]
    vs = [_stack_heads(x) for x in v]

    n = 2 * c
    ri = lax.broadcasted_iota(I32, (n, n), 0)
    ci = lax.broadcasted_iota(I32, (n, n), 1)
    same = (ri // c) == (ci // c)
    strict = same & (ci < ri)
    incl = same & (ci <= ri)
    sc = _mm([jnp.concatenate([x, y], axis=0) for x, y in zip(at, rt)],
             [jnp.concatenate([x, y], axis=0) for x, y in zip(bt, kt)], _NT, ps["scores"])
    a_ab = [jnp.where(strict, x[0:n, 0:n], 0.0) for x in sc]
    a_ak = [jnp.where(strict, x[0:n, n:2 * n], 0.0) for x in sc]
    a_rb = [jnp.where(incl, x[n:2 * n, 0:n], 0.0) for x in sc]
    a_rk = [jnp.where(incl, x[n:2 * n, n:2 * n], 0.0) for x in sc]

    av = _mm(a_ak, vs, _NN, ps["apply"])
    ark_v = _mm(a_rk, vs, _NN, ps["apply"])
    vk = _mm(vs, kt, _TN, ps["state"])
    tinv = _unit_lower_inverse(a_ab)
    pq = _mm(tinv, [jnp.concatenate([x, y], axis=1) for x, y in zip(at, av)], _NN, ps["apply"])
    rb = _mm(a_rb, pq, _NN, ps["apply"])
    r2 = [x + y[:, 0:LANES] for x, y in zip(rt, rb)]
    y0 = [x[:, LANES:2 * LANES] + y for x, y in zip(rb, ark_v)]
    y = [x + z for x, z in zip(_mm(r2, s0, _NT, ps["chain"]), y0)]
    eye = (ri == ci).astype(F32)
    pqb = _mm(pq, bt, _TN, ps["state"])
    wc = [w[c - 1:c, :] for w in wcur]
    m_k = [(eye + x[0:LANES]) * w for x, w in zip(pqb, wc)]
    n_k = [(x[LANES:2 * LANES] + z) * w for x, z, w in zip(pqb, vk, wc)]
    s_new = [x + z for x, z in zip(_mm(s0, m_k, _NN, ps["chain"]), n_k)]

    yp = [x[0:c, :] + x[c:n, :] for x in y]
    lane = lax.broadcasted_iota(I32, (LANES, LANES), 0) // HEAD_DIM
    lane_t = lax.broadcasted_iota(I32, (LANES, LANES), 1) // HEAD_DIM
    hsum = [(lane == lane_t).astype(F32)] * npair
    inv_n = 1.0 / HEAD_DIM
    mu = [x * inv_n for x in _mm(yp, hsum, _NN, ps["stats"])]
    bonus = _mm([x * z * w for x, z, w in zip(r, k, rk)], hsum, _NN, ps["stats"])
    dev = [x - z for x, z in zip(yp, mu)]
    var = [x * inv_n for x in _mm([x * x for x in dev], hsum, _NN, ps["stats"])]
    out = [(dv * lax.rsqrt(vr + GROUPNORM_EPS) * lg + lb + bo * vv) * gg
           for dv, vr, lg, lb, bo, vv, gg in zip(dev, var, lng, lnb, bonus, v, g)]
    return out, s_new


def _rwkv_scan_kernel(r_ref, k_ref, v_ref, kk_ref, b_ref, lw_ref, g_ref, rk_ref, lng_ref, lnb_ref,
                      o_ref, s_scr):
    @pl.when(pl.program_id(2) == 0)
    def _():
        s_scr[...] = jnp.zeros_like(s_scr)

    sls = [slice(j * LANES, (j + 1) * LANES) for j in range(SCAN_PAIRS)]
    split = lambda ref: [ref[:, sl] for sl in sls]
    out, s_new = _scan_pairs(split(r_ref), split(k_ref), split(v_ref), split(kk_ref), split(b_ref), split(lw_ref),
                             split(g_ref), split(rk_ref), split(lng_ref), split(lnb_ref),
                             [s_scr[j] for j in range(SCAN_PAIRS)])
    for j in range(SCAN_PAIRS):
        o_ref[:, sls[j]] = out[j]
        s_scr[j] = s_new[j]


def _rwkv_scan(r, k, v, kk, bvec, lw, g, rk, lng, lnb, seq):
    m = r.shape[0]
    c = SCAN_CHUNK
    nc = seq // c
    width = SCAN_PAIRS * LANES
    blk = pl.BlockSpec((c, width), lambda b, h, i: (b * nc + i, h))
    prm = pl.BlockSpec((1, width), lambda b, h, i: (0, h))
    return pl.pallas_call(
        _rwkv_scan_kernel,
        grid=(m // seq, RWKV_WIDTH // width, nc),
        in_specs=[blk] * 7 + [prm] * 3,
        out_specs=blk,
        out_shape=jax.ShapeDtypeStruct((m, RWKV_WIDTH), F32),
        scratch_shapes=[pltpu.VMEM((SCAN_PAIRS, LANES, LANES), F32)],
        compiler_params=_cparams(("arbitrary", "arbitrary", "arbitrary"), VMEM_LIMIT),
        name="rwkv_scan",
    )(r, k, v, kk, bvec, lw, g, rk.reshape(1, -1), lng.reshape(1, -1), lnb.reshape(1, -1))


def _out_proj_kernel(x_ref, ya_ref, yb_ref, wa_ref, wb_ref, gt_ref, g2_ref, sh_ref, sc_ref, x1_ref, h_ref):
    y = _dot(ya_ref[...].astype(BF16), wa_ref[...]) + _dot(yb_ref[...].astype(BF16), wb_ref[...])
    x1 = x_ref[...] + gt_ref[0] * y
    x1_ref[...] = x1
    hn = x1 * lax.rsqrt(jnp.mean(x1 * x1, axis=-1, keepdims=True) + NORM_EPS) * g2_ref[...]
    h_ref[...] = hn * (1.0 + sc_ref[0]) + sh_ref[0]


def _out_proj(xf, ya, yb, w_bf, gt, g2, sh, sc, seq):
    m, d = xf.shape
    tm = min(256, seq)
    per_b = seq // tm
    half = GMLP_WIDTH
    bspec = pl.BlockSpec((1, 1, d), lambda i: (i // per_b, 0, 0))
    return pl.pallas_call(
        _out_proj_kernel,
        grid=(m // tm,),
        in_specs=[pl.BlockSpec((tm, d), lambda i: (i, 0)),
                  pl.BlockSpec((tm, half), lambda i: (i, 0)),
                  pl.BlockSpec((tm, half), lambda i: (i, 0)),
                  pl.BlockSpec((half, d), lambda i: (0, 0)),
                  pl.BlockSpec((half, d), lambda i: (1, 0)),
                  bspec,
                  pl.BlockSpec((1, d), lambda i: (0, 0)),
                  bspec, bspec],
        out_specs=[pl.BlockSpec((tm, d), lambda i: (i, 0)), pl.BlockSpec((tm, d), lambda i: (i, 0))],
        out_shape=[jax.ShapeDtypeStruct((m, d), F32), jax.ShapeDtypeStruct((m, d), F32)],
        compiler_params=_cparams(("arbitrary",), VMEM_LIMIT),
        name="out_proj",
    )(xf, ya, yb, w_bf, w_bf, gt, g2.reshape(1, d), sh, sc)


def _q_scores_kernel(h_ref, wq_ref, keys_ref, o_ref):
    q = _dot(h_ref[...].astype(BF16), wq_ref[...]).astype(BF16)
    for hp in range(2 * PEER_HEADS):
        o_ref[hp * PEER_KEYS:(hp + 1) * PEER_KEYS, :] = _dot_nt(
            keys_ref[hp], q[:, hp * LANES:(hp + 1) * LANES])


def _q_scores(h, wq_bf, keys_bf):
    m, d = h.shape
    tm = 256
    n = wq_bf.shape[1]
    return pl.pallas_call(
        _q_scores_kernel,
        grid=(m // tm,),
        in_specs=[pl.BlockSpec((tm, d), lambda i: (i, 0)),
                  pl.BlockSpec((d, n), lambda i: (0, 0)),
                  pl.BlockSpec((2 * PEER_HEADS, PEER_KEYS, LANES), lambda i: (0, 0, 0))],
        out_specs=pl.BlockSpec((n, tm), lambda i: (0, i)),
        out_shape=jax.ShapeDtypeStruct((n, m), F32),
        compiler_params=_cparams(("arbitrary",), VMEM_LIMIT),
        name="q_scores",
    )(h, wq_bf, keys_bf)


def _extract_topk(vals, k):
    rows = vals.shape[0]
    iota = lax.broadcasted_iota(I32, vals.shape, 0)
    tops, poss = [], []
    for _ in range(k):
        m = jnp.max(vals, axis=0)
        pos = jnp.min(jnp.where(vals == m[None], iota, rows), axis=0)
        vals = jnp.where(iota == pos[None], -jnp.inf, vals)
        tops.append(m)
        poss.append(pos)
    return tops, poss


def _topk_kernel(s_ref, e_ref, g_ref, st_scr, it_scr):
    k = PEER_TOPK

    def half_body(hp, carry):
        vals = s_ref[pl.ds(pl.multiple_of(hp * PEER_KEYS, PEER_KEYS), PEER_KEYS)]
        tops, poss = _extract_topk(vals, k)
        for j in range(k):
            st_scr[hp, j] = tops[j]
            it_scr[hp, j] = poss[j]
        return carry

    lax.fori_loop(0, 2 * PEER_HEADS, half_body, 0)

    def head_body(h, carry):
        s1 = [st_scr[2 * h, j] for j in range(k)]
        s2 = [st_scr[2 * h + 1, j] for j in range(k)]
        i1 = [it_scr[2 * h, j] for j in range(k)]
        i2 = [it_scr[2 * h + 1, j] for j in range(k)]
        cand = jnp.stack([s1[a] + s2[b] for a in range(k) for b in range(k)], axis=0)
        tops, poss = _extract_topk(cand, k)
        exps = [jnp.exp(t - tops[0]) for t in tops]
        den = exps[0]
        for t in exps[1:]:
            den = den + t
        for j in range(k):
            pa = poss[j] >> 4
            pb = poss[j] & (k - 1)
            ia = jnp.zeros_like(pa)
            ib = jnp.zeros_like(pa)
            for a in range(k):
                ia = jnp.where(pa == a, i1[a], ia)
                ib = jnp.where(pb == a, i2[a], ib)
            e_ref[h * k + j] = ia * PEER_KEYS + ib
            g_ref[h * k + j] = exps[j] / den
        return carry

    lax.fori_loop(0, PEER_HEADS, head_body, 0)


def _topk(scores3):
    n, nb, _ = scores3.shape
    blk = pl.BlockSpec((N_SEL, SUBLANES, LANES), lambda i: (0, i, 0))
    return pl.pallas_call(
        _topk_kernel,
        grid=(nb // SUBLANES,),
        in_specs=[pl.BlockSpec((n, SUBLANES, LANES), lambda i: (0, i, 0))],
        out_specs=[blk, blk],
        out_shape=[jax.ShapeDtypeStruct((N_SEL, nb, LANES), I32), jax.ShapeDtypeStruct((N_SEL, nb, LANES), F32)],
        scratch_shapes=[pltpu.VMEM((2 * PEER_HEADS, PEER_TOPK, SUBLANES, LANES), F32),
                        pltpu.VMEM((2 * PEER_HEADS, PEER_TOPK, SUBLANES, LANES), I32)],
        compiler_params=_cparams(("arbitrary",), VMEM_LIMIT),
        name="topk",
    )(scores3)


def _route_kernel(e_ref, g_ref, ids_ref, gs_ref, st_ref, q_scr):
    zero = jnp.zeros((SUBLANES, LANES), I32)
    counts = [zero] * N_STRIPS
    for kk in range(N_SEL):
        strip = e_ref[kk] >> STRIP_SHIFT
        rank = zero
        for s in range(N_STRIPS):
            hit = strip == s
            rank = jnp.where(hit, counts[s], rank)
            counts[s] = counts[s] + hit.astype(I32)
        q_scr[kk] = rank
    start = zero
    starts = []
    for s in range(N_STRIPS):
        starts.append(start)
        st_ref[s] = start
        start = start + counts[s]
    st_ref[N_STRIPS] = start

    def fix(kk, carry):
        e = e_ref[kk]
        strip = e >> STRIP_SHIFT
        base = zero
        for s in range(N_STRIPS):
            base = jnp.where(strip == s, starts[s], base)
        q_scr[kk] = q_scr[kk] + base
        return carry

    lax.fori_loop(0, N_SEL, fix, 0)

    def place(q, carry):
        acc_i = zero
        acc_g = jnp.zeros((SUBLANES, LANES), F32)
        for kk in range(N_SEL):
            hit = q_scr[kk] == q
            acc_i = jnp.where(hit, e_ref[kk], acc_i)
            acc_g = jnp.where(hit, g_ref[kk], acc_g)
        ids_ref[q] = (acc_i & (STRIP - 1)) * TABLE_ROWS
        gs_ref[q] = acc_g
        return carry

    lax.fori_loop(0, N_SEL, place, 0)


def _route(e3, g3):
    _, nb, _ = e3.shape
    blk = pl.BlockSpec((N_SEL, SUBLANES, LANES), lambda i: (0, i, 0))
    sblk = pl.BlockSpec((N_STRIPS + 1, SUBLANES, LANES), lambda i: (0, i, 0))
    return pl.pallas_call(
        _route_kernel,
        grid=(nb // SUBLANES,),
        in_specs=[blk, blk],
        out_specs=[blk, blk, sblk],
        out_shape=[jax.ShapeDtypeStruct((N_SEL, nb, LANES), I32), jax.ShapeDtypeStruct((N_SEL, nb, LANES), F32),
                   jax.ShapeDtypeStruct((N_STRIPS + 1, nb, LANES), I32)],
        scratch_shapes=[pltpu.VMEM((N_SEL, SUBLANES, LANES), I32)],
        compiler_params=_cparams(("arbitrary",), VMEM_LIMIT),
        name="route",
    )(e3, g3)


def _cast_kernel(t_ref, o_ref):
    o_ref[...] = t_ref[...].astype(BF16)


def _cast_bf16(table):
    n, d = table.shape
    tr = 512
    return pl.pallas_call(
        _cast_kernel,
        grid=(n // tr,),
        in_specs=[pl.BlockSpec((tr, d), lambda i: (i, 0))],
        out_specs=pl.BlockSpec((tr, d), lambda i: (i, 0)),
        out_shape=jax.ShapeDtypeStruct((n, d), BF16),
        compiler_params=_cparams(("arbitrary",), VMEM_LIMIT),
        name="cast",
    )(table)


def _pack_kernel(u_ref, v_ref, o_ref):
    tr = u_ref.shape[0]
    words = lambda ref: pltpu.bitcast(ref[...].reshape(tr * 2 * SUBLANES, LANES), I32).reshape(tr, SUBLANES, LANES)
    o_ref[:, 0] = words(u_ref)
    o_ref[:, 1] = words(v_ref)


def _pack(u_bf, v_bf):
    n, d = u_bf.shape
    tr = 256
    tile = lambda t: t.reshape(n, d // LANES, LANES)
    blk = pl.BlockSpec((tr, d // LANES, LANES), lambda i: (i, 0, 0))
    return pl.pallas_call(
        _pack_kernel,
        grid=(n // tr,),
        in_specs=[blk, blk],
        out_specs=pl.BlockSpec((tr, 2, SUBLANES, LANES), lambda i: (i, 0, 0, 0)),
        out_shape=jax.ShapeDtypeStruct((n, 2, SUBLANES, LANES), I32),
        compiler_params=_cparams(("arbitrary",), VMEM_LIMIT),
        name="pack",
    )(tile(u_bf), tile(v_bf))


def _fold8(xs):
    sub = lax.broadcasted_iota(I32, (SUBLANES, LANES), 0)
    xs = [xs[t ^ 3] for t in range(8)]
    low = sub < 4
    t4 = []
    for i in range(4):
        keep = jnp.where(low, xs[i], xs[i + 4])
        move = jnp.where(low, xs[i + 4], xs[i])
        t4.append(keep + pltpu.roll(move, 4, 0))
    m2 = (sub & 2) != 0
    t2 = [jnp.where(m2, t4[i] + pltpu.roll(t4[i], 2, 0), t4[i + 2] + pltpu.roll(t4[i + 2], 6, 0)) for i in range(2)]
    m1 = (sub & 1) != 0
    return jnp.where(m1, t2[0] + pltpu.roll(t2[0], 1, 0), t2[1] + pltpu.roll(t2[1], 7, 0))


def _split_dot(x, w_bf):
    hi = x.astype(BF16)
    lo = (x - hi.astype(F32)).astype(BF16)
    return _dot(hi, w_bf) + _dot(lo, w_bf)


def _peer_gather(sub, rd, s, ids_ref, st_ref, gs_ref, h_ref, tab_ref, stage_ref):
    lane = lax.broadcasted_iota(I32, (SUBLANES, LANES), 1)
    subl = lax.broadcasted_iota(I32, (SUBLANES, LANES), 0)
    parts = [[None] * SUB_TOK for _ in range(ROUND_SLOTS)]
    g8 = jnp.zeros((SUBLANES, LANES), F32)
    nst = N_STRIPS + 1
    st0 = sub * (SUB_TOK * nst) + s
    h8 = h_ref.at[pl.ds(sub * SUB_TOK, SUB_TOK)]
    gs8 = gs_ref.at[pl.ds(pl.multiple_of(sub * SUB_TOK, SUB_TOK), SUB_TOK)]
    for t in range(SUB_TOK):
        beg = st_ref[0, 0, st0 + t * nst]
        end = st_ref[0, 0, st0 + (t * nst + 1)]
        base = beg + rd * ROUND_SLOTS
        h_t = h8[t]
        ids0 = (sub * SUB_TOK + t) * N_SEL + base
        for j in range(ROUND_SLOTS):
            row = pl.multiple_of(ids_ref[0, 0, ids0 + j], TABLE_ROWS)
            u = pltpu.bitcast(tab_ref[0, pl.ds(row, SUBLANES), :], BF16)
            p = u.astype(F32) * h_t
            parts[j][t] = p[0:SUBLANES] + p[SUBLANES:2 * SUBLANES]
            stage_ref[t * ROUND_SLOTS + j] = tab_ref[0, pl.ds(row + SUBLANES, SUBLANES), :]
        gates = jnp.broadcast_to(gs8[pl.ds(t, 1), :], (SUBLANES, LANES))
        gates = pltpu.roll(gates, (N_SEL - base) & (N_SEL - 1), 1)
        g8 = jnp.where((subl == t) & (lane < end - base), gates, g8)
    return jnp.concatenate([_fold8(parts[j]) for j in range(ROUND_SLOTS)], axis=0), g8


def _peer_activate(red, g8):
    lane = lax.broadcasted_iota(I32, (SUBLANES, LANES), 1)
    z = _split_dot(red, jnp.ones((LANES, LANES), BF16))
    group = lane // (LANES // ROUND_SLOTS)
    pre = jnp.zeros((SUBLANES, LANES), F32)
    for j in range(ROUND_SLOTS):
        pre = jnp.where(group == j, z[j * SUBLANES:(j + 1) * SUBLANES], pre)
    ri = lax.broadcasted_iota(I32, (LANES, LANES), 0)
    ci = lax.broadcasted_iota(I32, (LANES, LANES), 1)
    spread = (ri == ci // (LANES // ROUND_SLOTS)).astype(BF16)
    return _gelu(pre) * _split_dot(g8, spread)


def _peer_apply(psub, act, stage_ref, o_ref):
    row16 = lax.broadcasted_iota(I32, (2 * SUBLANES, LANES), 0)
    lane16 = lax.broadcasted_iota(I32, (2 * SUBLANES, LANES), 1)
    diag = (lane16 & (2 * SUBLANES - 1)) == row16
    o8 = o_ref.at[pl.ds(psub * SUB_TOK, SUB_TOK)]
    for t in range(SUB_TOK):
        a = jnp.where(diag, jnp.broadcast_to(act[t:t + 1, :], (2 * SUBLANES, LANES)), 0.0).astype(BF16)
        w = pltpu.bitcast(stage_ref[pl.ds(t * ROUND_SLOTS, ROUND_SLOTS)].reshape(ROUND_SLOTS * SUBLANES, LANES), BF16)
        o8[t] = o8[t] + _dot(a, w)


def _peer_kernel(ids_ref, st_ref, gs_ref, h_ref, tab_ref, o_ref, stage_a, stage_b, stage_c, nr_ref):
    s = pl.program_id(0)
    nsub = TOK_TILE // SUB_TOK
    o_ref[...] = jnp.zeros_like(o_ref)
    total = jnp.int32(0)
    for sub in range(nsub):
        cmax = jnp.int32(0)
        for t in range(SUB_TOK):
            tok = sub * SUB_TOK + t
            cmax = jnp.maximum(cmax, st_ref[0, 0, tok * (N_STRIPS + 1) + s + 1] - st_ref[0, 0, tok * (N_STRIPS + 1) + s])
        nr = jnp.maximum((cmax + ROUND_SLOTS - 1) // ROUND_SLOTS, 1)
        nr_ref[sub] = nr
        total = total + nr
    done_rd = N_SEL // ROUND_SLOTS

    def step(state, stage_new, stage_old):
        sub, rd, sub_mid, sub_old, red, g8, act_old = state
        act = _peer_activate(red, g8)
        _peer_apply(sub_old, act_old, stage_old, o_ref)
        live = sub < nsub
        sub_c = jnp.minimum(sub, nsub - 1)
        red, g8 = _peer_gather(sub_c, jnp.where(live, rd, done_rd), s, ids_ref, st_ref, gs_ref, h_ref, tab_ref, stage_new)
        last = rd + 1 >= nr_ref[sub_c]
        return (jnp.where(last, sub + 1, sub), jnp.where(last, 0, rd + 1), sub_c, sub_mid, red, g8, act)

    def body(i, state):
        state = step(state, stage_a, stage_b)
        state = step(state, stage_b, stage_c)
        return step(state, stage_c, stage_a)

    stage_b[...] = jnp.zeros_like(stage_b)
    stage_c[...] = jnp.zeros_like(stage_c)
    zero = jnp.int32(0)
    init = (zero, zero, zero, zero, jnp.zeros((SUB_TOK * ROUND_SLOTS, LANES), F32), jnp.zeros((SUBLANES, LANES), F32),
            jnp.zeros((SUBLANES, LANES), F32))
    lax.fori_loop(0, (total + 4) // 3, body, init)


def _peer(ids_t, st_t, gs_t, h3, table):
    ntile = ids_t.shape[0]
    m = h3.shape[0]
    smem = lambda width: pl.BlockSpec((1, 1, width), lambda s, j: (j, 0, 0), memory_space=pltpu.SMEM)
    stage = pltpu.VMEM((SUB_TOK * ROUND_SLOTS, SUBLANES, LANES), I32)
    return pl.pallas_call(
        _peer_kernel,
        grid=(N_STRIPS, ntile),
        in_specs=[smem(ids_t.shape[2]), smem(TOK_TILE * (N_STRIPS + 1)),
                  pl.BlockSpec((TOK_TILE, N_SEL), lambda s, j: (j, 0)),
                  pl.BlockSpec((TOK_TILE, 2 * SUBLANES, LANES), lambda s, j: (j, 0, 0)),
                  pl.BlockSpec((1, STRIP * TABLE_ROWS, LANES), lambda s, j: (s, 0, 0))],
        out_specs=pl.BlockSpec((None, TOK_TILE, 2 * SUBLANES, LANES), lambda s, j: (s, j, 0, 0)),
        out_shape=jax.ShapeDtypeStruct((N_STRIPS, m, 2 * SUBLANES, LANES), F32),
        scratch_shapes=[stage, stage, stage, pltpu.SMEM((TOK_TILE // SUB_TOK,), I32)],
        compiler_params=_cparams(("arbitrary", "arbitrary"), VMEM_LIMIT),
        name="peer",
    )(ids_t, st_t, gs_t, h3, table)


def _final_kernel(x_ref, p_ref, gt_ref, g_ref, o_ref):
    acc = p_ref[0]
    for s in range(1, N_STRIPS):
        acc = acc + p_ref[s]
    x2 = x_ref[...] + gt_ref[0] * acc
    o_ref[...] = x2 * lax.rsqrt(jnp.mean(x2 * x2, axis=-1, keepdims=True) + NORM_EPS) * g_ref[...]


def _final(x1, parts, gt, g, seq):
    m, d = x1.shape
    tm = min(128, seq)
    per_b = seq // tm
    return pl.pallas_call(
        _final_kernel,
        grid=(m // tm,),
        in_specs=[pl.BlockSpec((tm, d), lambda i: (i, 0)),
                  pl.BlockSpec((N_STRIPS, tm, d), lambda i: (0, i, 0)),
                  pl.BlockSpec((1, 1, d), lambda i: (i // per_b, 0, 0)),
                  pl.BlockSpec((1, d), lambda i: (0, 0))],
        out_specs=pl.BlockSpec((tm, d), lambda i: (i, 0)),
        out_shape=jax.ShapeDtypeStruct((m, d), F32),
        compiler_params=_cparams(("arbitrary",), VMEM_LIMIT),
        name="final",
    )(x1, parts, gt, g.reshape(1, d))


def _token_mixing(xf, seq, sh1, sc1, gt1, norm1_g, w_in, gmlp_v_g, gmlp_v_b, gmlp_ws, gmlp_bs, gmlp_out_g,
                  rwkv_mu, rwkv_w0, rwkv_w2, rwkv_a0, rwkv_a2, rwkv_g2, rwkv_kk, rwkv_ka, rwkv_rk,
                  rwkv_ln_g, rwkv_ln_b):
    w_pad = jnp.pad(w_in, ((0, 0), (0, IN_PAD - IN_WIDTH))).astype(BF16)
    p = _in_proj(xf, norm1_g, sh1, sc1, w_pad, seq)
    y_a = _gmlp(p, gmlp_v_g, gmlp_v_b, gmlp_ws, gmlp_bs, gmlp_out_g)
    r, k, v, kk, bvec, lw, g = _rwkv_prep(p, rwkv_mu, rwkv_w0, rwkv_w2, rwkv_a0, rwkv_a2, rwkv_g2,
                                          rwkv_kk, rwkv_ka, seq)
    y_b = _rwkv_scan(r, k, v, kk, bvec, lw, g, rwkv_rk, rwkv_ln_g, rwkv_ln_b, seq)
    return y_a, y_b


def _peer_ffn(h2, peer_wq, peer_keys, peer_u, peer_v):
    m, d = h2.shape
    keys_bf = peer_keys.reshape(2 * PEER_HEADS, PEER_KEYS, LANES).astype(BF16)
    scores = _q_scores(h2, peer_wq.astype(BF16), keys_bf)
    e3, g3 = _topk(scores.reshape(scores.shape[0], m // LANES, LANES))
    ids3, gs3, st3 = _route(e3, g3)
    table = _pack(_cast_bf16(peer_u), _cast_bf16(peer_v))
    parts = _peer(*_peer_operands(ids3, gs3, st3, h2, table))
    return parts.reshape(N_STRIPS, m, d)


def _peer_operands(ids3, gs3, st3, h2, table):
    m, d = h2.shape
    ntile = m // TOK_TILE
    tok_major = lambda a_: a_.reshape(a_.shape[0], m).T
    ids_t = jnp.pad(tok_major(ids3).reshape(ntile, 1, TOK_TILE * N_SEL), ((0, 0), (0, 0), (0, IDS_PAD)))
    st_t = tok_major(st3).reshape(ntile, 1, TOK_TILE * (N_STRIPS + 1))
    return (ids_t, st_t, tok_major(gs3), h2.reshape(m, 2 * SUBLANES, LANES),
            table.reshape(N_STRIPS, STRIP * TABLE_ROWS, LANES))


def kernel(x, c, ada_w, ada_b, norm1_g, w_in, gmlp_v_g, gmlp_v_b, gmlp_ws, gmlp_bs, gmlp_out_g, rwkv_mu, rwkv_w0, rwkv_w2, rwkv_a0, rwkv_a2, rwkv_g2, rwkv_kk, rwkv_ka, rwkv_rk, rwkv_ln_g, rwkv_ln_b, w_out, norm2_g, peer_wq, peer_keys, peer_u, peer_v, final_g):
    bsz, seq, d = x.shape
    depth = ada_w.shape[0]
    xf = x.reshape(bsz * seq, d)
    for l in range(depth):
        mod = _mod(c, ada_w[l], ada_b[l])
        sh1, sc1, gt1, sh2, sc2, gt2 = [t.reshape(bsz, 1, d) for t in jnp.split(mod, 6, axis=-1)]
        y_a, y_b = _token_mixing(xf, seq, sh1, sc1, gt1, norm1_g[l], w_in[l], gmlp_v_g[l], gmlp_v_b[l],
                                 gmlp_ws[l], gmlp_bs[l], gmlp_out_g[l], rwkv_mu[l], rwkv_w0[l], rwkv_w2[l],
                                 rwkv_a0[l], rwkv_a2[l], rwkv_g2[l], rwkv_kk[l], rwkv_ka[l], rwkv_rk[l],
                                 rwkv_ln_g[l], rwkv_ln_b[l])
        x1, h2 = _out_proj(xf, y_a, y_b, w_out[l].astype(BF16), gt1, norm2_g[l], sh2, sc2, seq)
        parts = _peer_ffn(h2, peer_wq[l], peer_keys[l], peer_u[l], peer_v[l])
        last = l == depth - 1
        assert last, "multi-layer stacking needs an un-normalised residual output"
        xf = _final(x1, parts, gt2, final_g, seq)
    return xf.reshape(bsz, seq, d)
```

```python
import functools

import jax
import jax.numpy as jnp
from jax import lax
from jax.experimental import pallas as pl
from jax.experimental.pallas import tpu as pltpu

F32 = jnp.float32
BF16 = jnp.bfloat16
I32 = jnp.int32
HI = lax.Precision.HIGHEST

D_MODEL = 2048
GMLP_WIDTH = 1024
RWKV_WIDTH = 1024
GMLP_BLOCK = 128
GMLP_HEADS = 8
CHUNK = 64
HEAD_DIM = 64
DECAY_LORA = 64
AAA_LORA = 64
GATE_LORA = 160
LORA_PAD = 512
IN_WIDTH = 5408
IN_PAD = 5632
PEER_KEYS = 128
PEER_HEADS = 8
PEER_TOPK = 16
N_SEL = PEER_HEADS * PEER_TOPK
N_EXPERTS = PEER_KEYS * PEER_KEYS
N_STRIPS = 4
STRIP = N_EXPERTS // N_STRIPS
STRIP_SHIFT = STRIP.bit_length() - 1
TABLE_ROWS = 16
NORM_EPS = 1e-6
LN_EPS = 1e-5
GROUPNORM_EPS = 64e-5
SCAN_CHUNK = 64
LANES = 128
SUBLANES = 8
TOK_TILE = 64
SUB_TOK = 8
ROUND_SLOTS = 8
IDS_PAD = 384
VMEM_LIMIT = 56 * 1024 * 1024


def _cparams(sem, vmem=None):
    return pltpu.CompilerParams(dimension_semantics=sem, vmem_limit_bytes=vmem)


def _gelu(x):
    return 0.5 * x * (1.0 + lax.erf(x * 0.7071067811865476))


def _dot(a, b, **kw):
    return jnp.dot(a, b, preferred_element_type=F32, **kw)


def _dot_nt(a, b, **kw):
    return lax.dot_general(a, b, (((1,), (1,)), ((), ())), preferred_element_type=F32, **kw)


def _dot_tn(a, b, **kw):
    return lax.dot_general(a, b, (((0,), (0,)), ((), ())), preferred_element_type=F32, **kw)


def _mod_kernel(c_ref, w_ref, b_ref, o_ref):
    c = c_ref[...]
    sc = (c * jax.nn.sigmoid(c)).astype(BF16)
    o_ref[...] = _dot(sc, w_ref[...].astype(BF16)) + b_ref[...]


def _mod(c, w, b):
    bsz, d = c.shape
    n = w.shape[1]
    tn = 1024
    return pl.pallas_call(
        _mod_kernel,
        grid=(n // tn,),
        in_specs=[pl.BlockSpec((bsz, d), lambda j: (0, 0)),
                  pl.BlockSpec((d, tn), lambda j: (0, j)),
                  pl.BlockSpec((1, tn), lambda j: (0, j))],
        out_specs=pl.BlockSpec((bsz, tn), lambda j: (0, j)),
        out_shape=jax.ShapeDtypeStruct((bsz, n), F32),
        compiler_params=_cparams(("arbitrary",), VMEM_LIMIT),
        name="mod",
    )(c, w, b.reshape(1, n))


def _in_proj_kernel(x_ref, g_ref, sh_ref, sc_ref, w_ref, o_ref, h_scr):
    @pl.when(pl.program_id(1) == 0)
    def _():
        x = x_ref[...]
        y = x * lax.rsqrt(jnp.mean(x * x, axis=-1, keepdims=True) + NORM_EPS) * g_ref[...]
        h_scr[...] = (y * (1.0 + sc_ref[0]) + sh_ref[0]).astype(BF16)

    o_ref[...] = _dot(h_scr[...], w_ref[...])


def _in_proj(xf, g, sh, sc, w_bf, seq):
    m, d = xf.shape
    n = w_bf.shape[1]
    tm, tn = min(1024, seq), 512
    per_b = seq // tm
    return pl.pallas_call(
        _in_proj_kernel,
        grid=(m // tm, n // tn),
        in_specs=[pl.BlockSpec((tm, d), lambda i, j: (i, 0)),
                  pl.BlockSpec((1, d), lambda i, j: (0, 0)),
                  pl.BlockSpec((1, 1, d), lambda i, j: (i // per_b, 0, 0)),
                  pl.BlockSpec((1, 1, d), lambda i, j: (i // per_b, 0, 0)),
                  pl.BlockSpec((d, tn), lambda i, j: (0, j))],
        out_specs=pl.BlockSpec((tm, tn), lambda i, j: (i, j)),
        out_shape=jax.ShapeDtypeStruct((m, n), F32),
        scratch_shapes=[pltpu.VMEM((tm, d), BF16)],
        compiler_params=_cparams(("arbitrary", "arbitrary"), VMEM_LIMIT),
        name="in_proj",
    )(xf, g.reshape(1, d), sh, sc, w_bf)


def _gmlp_kernel(u_ref, v_ref, vg_ref, vb_ref, ws_ref, bs_ref, og_ref, o_ref):
    gu = _gelu(u_ref[...])
    gv = _gelu(v_ref[...])
    mu = jnp.mean(gv, axis=-1, keepdims=True)
    var = jnp.mean(jnp.square(gv - mu), axis=-1, keepdims=True)
    vn = ((gv - mu) * lax.rsqrt(var + LN_EPS) * vg_ref[...] + vb_ref[...]).astype(BF16)
    ri = lax.broadcasted_iota(I32, (GMLP_BLOCK, GMLP_BLOCK), 0) // CHUNK
    ci = lax.broadcasted_iota(I32, (GMLP_BLOCK, GMLP_BLOCK), 1) // CHUNK
    keep = ci <= ri
    bs = bs_ref[...]
    parts = []
    for h in range(GMLP_HEADS):
        w = jnp.where(keep, ws_ref[h], 0.0).astype(BF16)
        z = _dot(w, vn[:, h * LANES:(h + 1) * LANES]) + bs[:, h:h + 1]
        parts.append(gu[:, h * LANES:(h + 1) * LANES] * z)
    y = jnp.concatenate(parts, axis=-1)
    o_ref[...] = y * lax.rsqrt(jnp.mean(y * y, axis=-1, keepdims=True) + NORM_EPS) * og_ref[...]


def _gmlp(p, vg, vb, ws, bs, og):
    m = p.shape[0]
    w = GMLP_WIDTH
    return pl.pallas_call(
        _gmlp_kernel,
        grid=(m // GMLP_BLOCK,),
        in_specs=[pl.BlockSpec((GMLP_BLOCK, w), lambda i: (i, 0)),
                  pl.BlockSpec((GMLP_BLOCK, w), lambda i: (i, 1)),
                  pl.BlockSpec((1, w), lambda i: (0, 0)),
                  pl.BlockSpec((1, w), lambda i: (0, 0)),
                  pl.BlockSpec((GMLP_HEADS, GMLP_BLOCK, GMLP_BLOCK), lambda i: (0, 0, 0)),
                  pl.BlockSpec((GMLP_BLOCK, GMLP_HEADS), lambda i: (0, 0)),
                  pl.BlockSpec((1, w), lambda i: (0, 0))],
        out_specs=pl.BlockSpec((GMLP_BLOCK, w), lambda i: (i, 0)),
        out_shape=jax.ShapeDtypeStruct((m, w), F32),
        compiler_params=_cparams(("arbitrary",), VMEM_LIMIT),
        name="gmlp",
    )(p, p, vg.reshape(1, w), vb.reshape(1, w), ws, bs.T, og.reshape(1, w))


def _shift_mix(p, prev_row, mu):
    rolled = pltpu.roll(p, 1, 0)
    first = lax.broadcasted_iota(I32, p.shape, 0) == 0
    shifted = jnp.where(first, prev_row, rolled)
    return p + (shifted - p) * mu


def _rwkv_prep_kernel(r_ref, k_ref, v_ref, l_ref, mur_ref, muk_ref, muv_ref, mul_ref,
                      w0_ref, w2_ref, a0_ref, a2_ref, g2_ref, kkw_ref, ka_ref, hsum_ref,
                      ro_ref, ko_ref, vo_ref, kko_ref, bo_ref, lw_ref, go_ref,
                      pr_scr, pk_scr, pv_scr, plr_scr):
    @pl.when(pl.program_id(1) == 0)
    def _():
        pr_scr[...] = jnp.zeros_like(pr_scr)
        pk_scr[...] = jnp.zeros_like(pk_scr)
        pv_scr[...] = jnp.zeros_like(pv_scr)
        plr_scr[...] = jnp.zeros_like(plr_scr)

    rows = r_ref.shape[0]
    r_raw, k_raw, v_raw, l_raw = r_ref[...], k_ref[...], v_ref[...], l_ref[...]
    r = _shift_mix(r_raw, pr_scr[...], mur_ref[...])
    k = _shift_mix(k_raw, pk_scr[...], muk_ref[...])
    v = _shift_mix(v_raw, pv_scr[...], muv_ref[...])
    lo = _shift_mix(l_raw, plr_scr[...], mul_ref[...])
    pr_scr[...] = r_raw[rows - 1:rows, :]
    pk_scr[...] = k_raw[rows - 1:rows, :]
    pv_scr[...] = v_raw[rows - 1:rows, :]
    plr_scr[...] = l_raw[rows - 1:rows, :]

    wpre = w0_ref[...] + _dot(jnp.tanh(lo).astype(BF16), w2_ref[...])
    w = -jax.nn.softplus(-wpre) - 0.5
    a = jax.nn.sigmoid(a0_ref[...] + _dot(lo.astype(BF16), a2_ref[...]))
    g = _dot(jax.nn.sigmoid(lo).astype(BF16), g2_ref[...])

    kks = k * kkw_ref[...]
    sq = kks * kks
    hsum = hsum_ref[...]
    ss = jnp.concatenate([_dot(sq[:, j * LANES:(j + 1) * LANES], hsum, precision=HI)
                          for j in range(RWKV_WIDTH // LANES)], axis=-1)
    kk = kks / jnp.maximum(jnp.sqrt(ss), 1e-12)
    ro_ref[...] = r
    ko_ref[...] = k * (1.0 + (a - 1.0) * ka_ref[...])
    vo_ref[...] = v
    kko_ref[...] = kk
    bo_ref[...] = kk * a
    lw_ref[...] = -jnp.exp(w)
    go_ref[...] = g


def _rwkv_prep(p, mu, w0, w2, a0, a2, g2, kkw, ka, seq):
    m = p.shape[0]
    w = RWKV_WIDTH
    ts = min(256, seq)
    nb = seq // ts
    lora_cols = DECAY_LORA + AAA_LORA + GATE_LORA
    mu_r, mu_k, mu_v = mu[0:w], mu[w:2 * w], mu[2 * w:3 * w]
    mu_l = jnp.pad(mu[3 * w:], (0, LORA_PAD - lora_cols))
    lane_head = jnp.arange(LANES) // HEAD_DIM
    hsum = (lane_head[:, None] == lane_head[None, :]).astype(F32)
    pad_rows = lambda t, at: jnp.pad(t, ((at, LORA_PAD - at - t.shape[0]), (0, 0))).astype(BF16)
    w2 = pad_rows(w2, 0)
    a2 = pad_rows(a2, DECAY_LORA)
    g2 = pad_rows(g2, DECAY_LORA + AAA_LORA)
    row = lambda a_: a_.reshape(1, -1)
    big = pl.BlockSpec((ts, w), lambda b, i: (b * nb + i, 0))
    const = lambda shape: pl.BlockSpec(shape, lambda b, i: (0,) * len(shape))
    outs = pl.pallas_call(
        _rwkv_prep_kernel,
        grid=(m // seq, nb),
        in_specs=[pl.BlockSpec((ts, w), lambda b, i: (b * nb + i, 2)),
                  pl.BlockSpec((ts, w), lambda b, i: (b * nb + i, 3)),
                  pl.BlockSpec((ts, w), lambda b, i: (b * nb + i, 4)),
                  pl.BlockSpec((ts, LORA_PAD), lambda b, i: (b * nb + i, 10)),
                  const((1, w)), const((1, w)), const((1, w)), const((1, LORA_PAD)),
                  const((1, w)), const((LORA_PAD, w)), const((1, w)), const((LORA_PAD, w)),
                  const((LORA_PAD, w)), const((1, w)), const((1, w)), const((LANES, LANES))],
        out_specs=[big] * 7,
        out_shape=[jax.ShapeDtypeStruct((m, w), F32)] * 7,
        scratch_shapes=[pltpu.VMEM((1, w), F32), pltpu.VMEM((1, w), F32), pltpu.VMEM((1, w), F32),
                        pltpu.VMEM((1, LORA_PAD), F32)],
        compiler_params=_cparams(("arbitrary", "arbitrary"), VMEM_LIMIT),
        name="rwkv_prep",
    )(p, p, p, p, row(mu_r), row(mu_k), row(mu_v), row(mu_l),
      row(w0), w2, row(a0), a2, g2, row(kkw), row(ka), hsum)
    return outs


def _stack_heads(x):
    lane = lax.broadcasted_iota(I32, x.shape, 1)
    return jnp.concatenate([jnp.where(lane < HEAD_DIM, x, 0.0), jnp.where(lane >= HEAD_DIM, x, 0.0)], axis=0)


def _mm(a_list, b_list, dims, passes):
    dn = (dims, ((), ()))
    dg = lambda x, y, **kw: lax.dot_general(x, y, dn, preferred_element_type=F32, **kw)
    if passes == 6:
        return [dg(a, b, precision=HI) for a, b in zip(a_list, b_list)]
    ah = [a.astype(BF16) for a in a_list]
    bh = [b.astype(BF16) for b in b_list]
    main = [dg(x, y) for x, y in zip(ah, bh)]
    if passes == 1:
        return main
    al = [(a - h.astype(F32)).astype(BF16) for a, h in zip(a_list, ah)]
    bl = [(b - h.astype(F32)).astype(BF16) for b, h in zip(b_list, bh)]
    c1 = [dg(x, y) for x, y in zip(ah, bl)]
    c2 = [dg(x, y) for x, y in zip(al, bh)]
    return [m + (p + q) for m, p, q in zip(main, c1, c2)]


_NN = ((1,), (0,))
_NT = ((1,), (1,))
_TN = ((0,), (0,))
SCAN_PASSES = {"cumsum": 3, "scores": 1, "inverse": 3, "apply": 1, "state": 1, "chain": 3, "stats": 1}
SCAN_PAIRS = 8


def _unit_lower_inverse(a_list):
    n = a_list[0].shape[0]
    ri = lax.broadcasted_iota(I32, (n, n), 0)
    ci = lax.broadcasted_iota(I32, (n, n), 1)
    eye = (ri == ci).astype(F32)
    diag_blk = (ri // 16) == (ci // 16)
    mm = lambda xs, ys: _mm(xs, ys, _NN, SCAN_PASSES["inverse"])
    grow = lambda ts, ps: [t + p for t, p in zip(ts, ps)]
    d = [jnp.where(diag_blk, a, 0.0) for a in a_list]
    low = [a - x for a, x in zip(a_list, d)]
    d2 = mm(d, d)
    d4 = mm(d2, d2)
    d8 = mm(d4, d4)
    td = [eye + x for x in d]
    td = grow(td, mm(td, d2))
    td = grow(td, mm(td, d4))
    td = grow(td, mm(td, d8))
    nn = mm(td, low)
    n2 = mm(nn, nn)
    x = [eye + t for t in nn]
    x = grow(x, mm(x, n2))
    return mm(x, td)


def _scan_pairs(r, k, v, kk, bb, lw, g, rk, lng, lnb, s0):
    c = SCAN_CHUNK
    ps = SCAN_PASSES
    npair = len(r)
    ti = lax.broadcasted_iota(I32, (c, c), 0)
    si = lax.broadcasted_iota(I32, (c, c), 1)
    tril = (si <= ti).astype(F32)
    cum = _mm([tril] * npair, lw, _NN, ps["cumsum"])
    wcur = [jnp.exp(x) for x in cum]
    winv = [jnp.exp(-x) for x in cum]
    wprev = [jnp.exp(x - y) for x, y in zip(cum, lw)]
    rt = [_stack_heads(x * w) for x, w in zip(r, wcur)]
    kt = [_stack_heads(x * w) for x, w in zip(k, winv)]
    bt = [_stack_heads(x * w) for x, w in zip(bb, winv)]
    at = [_stack_heads(-x * w) for x, w in zip(kk, wprev)]
    vs = [_stack_heads(x) for x in v]

    n = 2 * c
    ri = lax.broadcasted_iota(I32, (n, n), 0)
    ci = lax.broadcasted_iota(I32, (n, n), 1)
    same = (ri // c) == (ci // c)
    strict = same & (ci < ri)
    incl = same & (ci <= ri)
    sc = _mm([jnp.concatenate([x, y], axis=0) for x, y in zip(at, rt)],
             [jnp.concatenate([x, y], axis=0) for x, y in zip(bt, kt)], _NT, ps["scores"])
    a_ab = [jnp.where(strict, x[0:n, 0:n], 0.0) for x in sc]
    a_ak = [jnp.where(strict, x[0:n, n:2 * n], 0.0) for x in sc]
    a_rb = [jnp.where(incl, x[n:2 * n, 0:n], 0.0) for x in sc]
    a_rk = [jnp.where(incl, x[n:2 * n, n:2 * n], 0.0) for x in sc]

    av = _mm(a_ak, vs, _NN, ps["apply"])
    ark_v = _mm(a_rk, vs, _NN, ps["apply"])
    vk = _mm(vs, kt, _TN, ps["state"])
    tinv = _unit_lower_inverse(a_ab)
    pq = _mm(tinv, [jnp.concatenate([x, y], axis=1) for x, y in zip(at, av)], _NN, ps["apply"])
    rb = _mm(a_rb, pq, _NN, ps["apply"])
    r2 = [x + y[:, 0:LANES] for x, y in zip(rt, rb)]
    y0 = [x[:, LANES:2 * LANES] + y for x, y in zip(rb, ark_v)]
    y = [x + z for x, z in zip(_mm(r2, s0, _NT, ps["chain"]), y0)]
    eye = (ri == ci).astype(F32)
    pqb = _mm(pq, bt, _TN, ps["state"])
    wc = [w[c - 1:c, :] for w in wcur]
    m_k = [(eye + x[0:LANES]) * w for x, w in zip(pqb, wc)]
    n_k = [(x[LANES:2 * LANES] + z) * w for x, z, w in zip(pqb, vk, wc)]
    s_new = [x + z for x, z in zip(_mm(s0, m_k, _NN, ps["chain"]), n_k)]

    yp = [x[0:c, :] + x[c:n, :] for x in y]
    lane = lax.broadcasted_iota(I32, (LANES, LANES), 0) // HEAD_DIM
    lane_t = lax.broadcasted_iota(I32, (LANES, LANES), 1) // HEAD_DIM
    hsum = [(lane == lane_t).astype(F32)] * npair
    inv_n = 1.0 / HEAD_DIM
    mu = [x * inv_n for x in _mm(yp, hsum, _NN, ps["stats"])]
    bonus = _mm([x * z * w for x, z, w in zip(r, k, rk)], hsum, _NN, ps["stats"])
    dev = [x - z for x, z in zip(yp, mu)]
    var = [x * inv_n for x in _mm([x * x for x in dev], hsum, _NN, ps["stats"])]
    out = [(dv * lax.rsqrt(vr + GROUPNORM_EPS) * lg + lb + bo * vv) * gg
           for dv, vr, lg, lb, bo, vv, gg in zip(dev, var, lng, lnb, bonus, v, g)]
    return out, s_new


def _rwkv_scan_kernel(r_ref, k_ref, v_ref, kk_ref, b_ref, lw_ref, g_ref, rk_ref, lng_ref, lnb_ref,
                      o_ref, s_scr):
    @pl.when(pl.program_id(2) == 0)
    def _():
        s_scr[...] = jnp.zeros_like(s_scr)

    sls = [slice(j * LANES, (j + 1) * LANES) for j in range(SCAN_PAIRS)]
    split = lambda ref: [ref[:, sl] for sl in sls]
    out, s_new = _scan_pairs(split(r_ref), split(k_ref), split(v_ref), split(kk_ref), split(b_ref), split(lw_ref),
                             split(g_ref), split(rk_ref), split(lng_ref), split(lnb_ref),
                             [s_scr[j] for j in range(SCAN_PAIRS)])
    for j in range(SCAN_PAIRS):
        o_ref[:, sls[j]] = out[j]
        s_scr[j] = s_new[j]


def _rwkv_scan(r, k, v, kk, bvec, lw, g, rk, lng, lnb, seq):
    m = r.shape[0]
    c = SCAN_CHUNK
    nc = seq // c
    width = SCAN_PAIRS * LANES
    blk = pl.BlockSpec((c, width), lambda b, h, i: (b * nc + i, h))
    prm = pl.BlockSpec((1, width), lambda b, h, i: (0, h))
    return pl.pallas_call(
        _rwkv_scan_kernel,
        grid=(m // seq, RWKV_WIDTH // width, nc),
        in_specs=[blk] * 7 + [prm] * 3,
        out_specs=blk,
        out_shape=jax.ShapeDtypeStruct((m, RWKV_WIDTH), F32),
        scratch_shapes=[pltpu.VMEM((SCAN_PAIRS, LANES, LANES), F32)],
        compiler_params=_cparams(("arbitrary", "arbitrary", "arbitrary"), VMEM_LIMIT),
        name="rwkv_scan",
    )(r, k, v, kk, bvec, lw, g, rk.reshape(1, -1), lng.reshape(1, -1), lnb.reshape(1, -1))


def _out_proj_kernel(x_ref, ya_ref, yb_ref, wa_ref, wb_ref, gt_ref, g2_ref, sh_ref, sc_ref, x1_ref, h_ref):
    y = _dot(ya_ref[...].astype(BF16), wa_ref[...]) + _dot(yb_ref[...].astype(BF16), wb_ref[...])
    x1 = x_ref[...] + gt_ref[0] * y
    x1_ref[...] = x1
    hn = x1 * lax.rsqrt(jnp.mean(x1 * x1, axis=-1, keepdims=True) + NORM_EPS) * g2_ref[...]
    h_ref[...] = hn * (1.0 + sc_ref[0]) + sh_ref[0]


def _out_proj(xf, ya, yb, w_bf, gt, g2, sh, sc, seq):
    m, d = xf.shape
    tm = min(256, seq)
    per_b = seq // tm
    half = GMLP_WIDTH
    bspec = pl.BlockSpec((1, 1, d), lambda i: (i // per_b, 0, 0))
    return pl.pallas_call(
        _out_proj_kernel,
        grid=(m // tm,),
        in_specs=[pl.BlockSpec((tm, d), lambda i: (i, 0)),
                  pl.BlockSpec((tm, half), lambda i: (i, 0)),
                  pl.BlockSpec((tm, half), lambda i: (i, 0)),
                  pl.BlockSpec((half, d), lambda i: (0, 0)),
                  pl.BlockSpec((half, d), lambda i: (1, 0)),
                  bspec,
                  pl.BlockSpec((1, d), lambda i: (0, 0)),
                  bspec, bspec],
        out_specs=[pl.BlockSpec((tm, d), lambda i: (i, 0)), pl.BlockSpec((tm, d), lambda i: (i, 0))],
        out_shape=[jax.ShapeDtypeStruct((m, d), F32), jax.ShapeDtypeStruct((m, d), F32)],
        compiler_params=_cparams(("arbitrary",), VMEM_LIMIT),
        name="out_proj",
    )(xf, ya, yb, w_bf, w_bf, gt, g2.reshape(1, d), sh, sc)


def _q_scores_kernel(h_ref, wq_ref, keys_ref, o_ref):
    q = _dot(h_ref[...].astype(BF16), wq_ref[...]).astype(BF16)
    for hp in range(2 * PEER_HEADS):
        o_ref[hp * PEER_KEYS:(hp + 1) * PEER_KEYS, :] = _dot_nt(
            keys_ref[hp], q[:, hp * LANES:(hp + 1) * LANES])


def _q_scores(h, wq_bf, keys_bf):
    m, d = h.shape
    tm = 256
    n = wq_bf.shape[1]
    return pl.pallas_call(
        _q_scores_kernel,
        grid=(m // tm,),
        in_specs=[pl.BlockSpec((tm, d), lambda i: (i, 0)),
                  pl.BlockSpec((d, n), lambda i: (0, 0)),
                  pl.BlockSpec((2 * PEER_HEADS, PEER_KEYS, LANES), lambda i: (0, 0, 0))],
        out_specs=pl.BlockSpec((n, tm), lambda i: (0, i)),
        out_shape=jax.ShapeDtypeStruct((n, m), F32),
        compiler_params=_cparams(("arbitrary",), VMEM_LIMIT),
        name="q_scores",
    )(h, wq_bf, keys_bf)


def _extract_topk(vals, k):
    rows = vals.shape[0]
    iota = lax.broadcasted_iota(I32, vals.shape, 0)
    tops, poss = [], []
    for _ in range(k):
        m = jnp.max(vals, axis=0)
        pos = jnp.min(jnp.where(vals == m[None], iota, rows), axis=0)
        vals = jnp.where(iota == pos[None], -jnp.inf, vals)
        tops.append(m)
        poss.append(pos)
    return tops, poss


def _topk_kernel(s_ref, e_ref, g_ref, st_scr, it_scr):
    k = PEER_TOPK

    def half_body(hp, carry):
        vals = s_ref[pl.ds(pl.multiple_of(hp * PEER_KEYS, PEER_KEYS), PEER_KEYS)]
        tops, poss = _extract_topk(vals, k)
        for j in range(k):
            st_scr[hp, j] = tops[j]
            it_scr[hp, j] = poss[j]
        return carry

    lax.fori_loop(0, 2 * PEER_HEADS, half_body, 0)

    def head_body(h, carry):
        s1 = [st_scr[2 * h, j] for j in range(k)]
        s2 = [st_scr[2 * h + 1, j] for j in range(k)]
        i1 = [it_scr[2 * h, j] for j in range(k)]
        i2 = [it_scr[2 * h + 1, j] for j in range(k)]
        cand = jnp.stack([s1[a] + s2[b] for a in range(k) for b in range(k)], axis=0)
        tops, poss = _extract_topk(cand, k)
        exps = [jnp.exp(t - tops[0]) for t in tops]
        den = exps[0]
        for t in exps[1:]:
            den = den + t
        for j in range(k):
            pa = poss[j] >> 4
            pb = poss[j] & (k - 1)
            ia = jnp.zeros_like(pa)
            ib = jnp.zeros_like(pa)
            for a in range(k):
                ia = jnp.where(pa == a, i1[a], ia)
                ib = jnp.where(pb == a, i2[a], ib)
            e_ref[h * k + j] = ia * PEER_KEYS + ib
            g_ref[h * k + j] = exps[j] / den
        return carry

    lax.fori_loop(0, PEER_HEADS, head_body, 0)


def _topk(scores3):
    n, nb, _ = scores3.shape
    blk = pl.BlockSpec((N_SEL, SUBLANES, LANES), lambda i: (0, i, 0))
    return pl.pallas_call(
        _topk_kernel,
        grid=(nb // SUBLANES,),
        in_specs=[pl.BlockSpec((n, SUBLANES, LANES), lambda i: (0, i, 0))],
        out_specs=[blk, blk],
        out_shape=[jax.ShapeDtypeStruct((N_SEL, nb, LANES), I32), jax.ShapeDtypeStruct((N_SEL, nb, LANES), F32)],
        scratch_shapes=[pltpu.VMEM((2 * PEER_HEADS, PEER_TOPK, SUBLANES, LANES), F32),
                        pltpu.VMEM((2 * PEER_HEADS, PEER_TOPK, SUBLANES, LANES), I32)],
        compiler_params=_cparams(("arbitrary",), VMEM_LIMIT),
        name="topk",
    )(scores3)


def _route_kernel(e_ref, g_ref, ids_ref, gs_ref, st_ref, q_scr):
    zero = jnp.zeros((SUBLANES, LANES), I32)
    counts = [zero] * N_STRIPS
    for kk in range(N_SEL):
        strip = e_ref[kk] >> STRIP_SHIFT
        rank = zero
        for s in range(N_STRIPS):
            hit = strip == s
            rank = jnp.where(hit, counts[s], rank)
            counts[s] = counts[s] + hit.astype(I32)
        q_scr[kk] = rank
    start = zero
    starts = []
    for s in range(N_STRIPS):
        starts.append(start)
        st_ref[s] = start
        start = start + counts[s]
    st_ref[N_STRIPS] = start

    def fix(kk, carry):
        e = e_ref[kk]
        strip = e >> STRIP_SHIFT
        base = zero
        for s in range(N_STRIPS):
            base = jnp.where(strip == s, starts[s], base)
        q_scr[kk] = q_scr[kk] + base
        return carry

    lax.fori_loop(0, N_SEL, fix, 0)

    def place(q, carry):
        acc_i = zero
        acc_g = jnp.zeros((SUBLANES, LANES), F32)
        for kk in range(N_SEL):
            hit = q_scr[kk] == q
            acc_i = jnp.where(hit, e_ref[kk], acc_i)
            acc_g = jnp.where(hit, g_ref[kk], acc_g)
        ids_ref[q] = (acc_i & (STRIP - 1)) * TABLE_ROWS
        gs_ref[q] = acc_g
        return carry

    lax.fori_loop(0, N_SEL, place, 0)


def _route(e3, g3):
    _, nb, _ = e3.shape
    blk = pl.BlockSpec((N_SEL, SUBLANES, LANES), lambda i: (0, i, 0))
    sblk = pl.BlockSpec((N_STRIPS + 1, SUBLANES, LANES), lambda i: (0, i, 0))
    return pl.pallas_call(
        _route_kernel,
        grid=(nb // SUBLANES,),
        in_specs=[blk, blk],
        out_specs=[blk, blk, sblk],
        out_shape=[jax.ShapeDtypeStruct((N_SEL, nb, LANES), I32), jax.ShapeDtypeStruct((N_SEL, nb, LANES), F32),
                   jax.ShapeDtypeStruct((N_STRIPS + 1, nb, LANES), I32)],
        scratch_shapes=[pltpu.VMEM((N_SEL, SUBLANES, LANES), I32)],
        compiler_params=_cparams(("arbitrary",), VMEM_LIMIT),
        name="route",
    )(e3, g3)


def _cast_kernel(t_ref, o_ref):
    o_ref[...] = t_ref[...].astype(BF16)


def _cast_bf16(table):
    n, d = table.shape
    tr = 512
    return pl.pallas_call(
        _cast_kernel,
        grid=(n // tr,),
        in_specs=[pl.BlockSpec((tr, d), lambda i: (i, 0))],
        out_specs=pl.BlockSpec((tr, d), lambda i: (i, 0)),
        out_shape=jax.ShapeDtypeStruct((n, d), BF16),
        compiler_params=_cparams(("arbitrary",), VMEM_LIMIT),
        name="cast",
    )(table)


def _pack_kernel(u_ref, v_ref, o_ref):
    tr = u_ref.shape[0]
    words = lambda ref: pltpu.bitcast(ref[...].reshape(tr * 2 * SUBLANES, LANES), I32).reshape(tr, SUBLANES, LANES)
    o_ref[:, 0] = words(u_ref)
    o_ref[:, 1] = words(v_ref)


def _pack(u_bf, v_bf):
    n, d = u_bf.shape
    tr = 256
    tile = lambda t: t.reshape(n, d // LANES, LANES)
    blk = pl.BlockSpec((tr, d // LANES, LANES), lambda i: (i, 0, 0))
    return pl.pallas_call(
        _pack_kernel,
        grid=(n // tr,),
        in_specs=[blk, blk],
        out_specs=pl.BlockSpec((tr, 2, SUBLANES, LANES), lambda i: (i, 0, 0, 0)),
        out_shape=jax.ShapeDtypeStruct((n, 2, SUBLANES, LANES), I32),
        compiler_params=_cparams(("arbitrary",), VMEM_LIMIT),
        name="pack",
    )(tile(u_bf), tile(v_bf))


def _fold8(xs):
    sub = lax.broadcasted_iota(I32, (SUBLANES, LANES), 0)
    xs = [xs[t ^ 3] for t in range(8)]
    low = sub < 4
    t4 = []
    for i in range(4):
        keep = jnp.where(low, xs[i], xs[i + 4])
        move = jnp.where(low, xs[i + 4], xs[i])
        t4.append(keep + pltpu.roll(move, 4, 0))
    m2 = (sub & 2) != 0
    t2 = [jnp.where(m2, t4[i] + pltpu.roll(t4[i], 2, 0), t4[i + 2] + pltpu.roll(t4[i + 2], 6, 0)) for i in range(2)]
    m1 = (sub & 1) != 0
    return jnp.where(m1, t2[0] + pltpu.roll(t2[0], 1, 0), t2[1] + pltpu.roll(t2[1], 7, 0))


def _split_dot(x, w_bf):
    hi = x.astype(BF16)
    lo = (x - hi.astype(F32)).astype(BF16)
    return _dot(hi, w_bf) + _dot(lo, w_bf)


def _peer_gather(sub, rd, s, ids_ref, st_ref, gs_ref, h_ref, tab_ref, stage_ref):
    lane = lax.broadcasted_iota(I32, (SUBLANES, LANES), 1)
    subl = lax.broadcasted_iota(I32, (SUBLANES, LANES), 0)
    parts = [[None] * SUB_TOK for _ in range(ROUND_SLOTS)]
    g8 = jnp.zeros((SUBLANES, LANES), F32)
    nst = N_STRIPS + 1
    st0 = sub * (SUB_TOK * nst) + s
    h8 = h_ref.at[pl.ds(sub * SUB_TOK, SUB_TOK)]
    gs8 = gs_ref.at[pl.ds(pl.multiple_of(sub * SUB_TOK, SUB_TOK), SUB_TOK)]
    for t in range(SUB_TOK):
        beg = st_ref[0, 0, st0 + t * nst]
        end = st_ref[0, 0, st0 + (t * nst + 1)]
        base = beg + rd * ROUND_SLOTS
        h_t = h8[t]
        ids0 = (sub * SUB_TOK + t) * N_SEL + base
        for j in range(ROUND_SLOTS):
            row = pl.multiple_of(ids_ref[0, 0, ids0 + j], TABLE_ROWS)
            u = pltpu.bitcast(tab_ref[0, pl.ds(row, SUBLANES), :], BF16)
            p = u.astype(F32) * h_t
            parts[j][t] = p[0:SUBLANES] + p[SUBLANES:2 * SUBLANES]
            stage_ref[t * ROUND_SLOTS + j] = tab_ref[0, pl.ds(row + SUBLANES, SUBLANES), :]
        gates = jnp.broadcast_to(gs8[pl.ds(t, 1), :], (SUBLANES, LANES))
        gates = pltpu.roll(gates, (N_SEL - base) & (N_SEL - 1), 1)
        g8 = jnp.where((subl == t) & (lane < end - base), gates, g8)
    return jnp.concatenate([_fold8(parts[j]) for j in range(ROUND_SLOTS)], axis=0), g8


def _peer_activate(red, g8):
    lane = lax.broadcasted_iota(I32, (SUBLANES, LANES), 1)
    z = _split_dot(red, jnp.ones((LANES, LANES), BF16))
    group = lane // (LANES // ROUND_SLOTS)
    pre = jnp.zeros((SUBLANES, LANES), F32)
    for j in range(ROUND_SLOTS):
        pre = jnp.where(group == j, z[j * SUBLANES:(j + 1) * SUBLANES], pre)
    ri = lax.broadcasted_iota(I32, (LANES, LANES), 0)
    ci = lax.broadcasted_iota(I32, (LANES, LANES), 1)
    spread = (ri == ci // (LANES // ROUND_SLOTS)).astype(BF16)
    return _gelu(pre) * _split_dot(g8, spread)


def _peer_apply(psub, act, stage_ref, o_ref):
    row16 = lax.broadcasted_iota(I32, (2 * SUBLANES, LANES), 0)
    lane16 = lax.broadcasted_iota(I32, (2 * SUBLANES, LANES), 1)
    diag = (lane16 & (2 * SUBLANES - 1)) == row16
    o8 = o_ref.at[pl.ds(psub * SUB_TOK, SUB_TOK)]
    for t in range(SUB_TOK):
        a = jnp.where(diag, jnp.broadcast_to(act[t:t + 1, :], (2 * SUBLANES, LANES)), 0.0).astype(BF16)
        w = pltpu.bitcast(stage_ref[pl.ds(t * ROUND_SLOTS, ROUND_SLOTS)].reshape(ROUND_SLOTS * SUBLANES, LANES), BF16)
        o8[t] = o8[t] + _dot(a, w)


def _peer_kernel(ids_ref, st_ref, gs_ref, h_ref, tab_ref, acc_ref, o_ref, stage_a, stage_b, stage_c, nr_ref):
    s = pl.program_id(0)
    nsub = TOK_TILE // SUB_TOK
    o_ref[...] = acc_ref[...]
    total = jnp.int32(0)
    for sub in range(nsub):
        cmax = jnp.int32(0)
        for t in range(SUB_TOK):
            tok = sub * SUB_TOK + t
            cmax = jnp.maximum(cmax, st_ref[0, 0, tok * (N_STRIPS + 1) + s + 1] - st_ref[0, 0, tok * (N_STRIPS + 1) + s])
        nr = jnp.maximum((cmax + ROUND_SLOTS - 1) // ROUND_SLOTS, 1)
        nr_ref[sub] = nr
        total = total + nr
    done_rd = N_SEL // ROUND_SLOTS

    def step(state, stage_new, stage_old):
        sub, rd, sub_mid, sub_old, red, g8, act_old = state
        act = _peer_activate(red, g8)
        _peer_apply(sub_old, act_old, stage_old, o_ref)
        live = sub < nsub
        sub_c = jnp.minimum(sub, nsub - 1)
        red, g8 = _peer_gather(sub_c, jnp.where(live, rd, done_rd), s, ids_ref, st_ref, gs_ref, h_ref, tab_ref, stage_new)
        last = rd + 1 >= nr_ref[sub_c]
        return (jnp.where(last, sub + 1, sub), jnp.where(last, 0, rd + 1), sub_c, sub_mid, red, g8, act)

    def body(i, state):
        state = step(state, stage_a, stage_b)
        state = step(state, stage_b, stage_c)
        return step(state, stage_c, stage_a)

    stage_b[...] = jnp.zeros_like(stage_b)
    stage_c[...] = jnp.zeros_like(stage_c)
    zero = jnp.int32(0)
    init = (zero, zero, zero, zero, jnp.zeros((SUB_TOK * ROUND_SLOTS, LANES), F32), jnp.zeros((SUBLANES, LANES), F32),
            jnp.zeros((SUBLANES, LANES), F32))
    lax.fori_loop(0, (total + 4) // 3, body, init)


def _peer(ids_t, st_t, gs_t, h3, table):
    ntile = ids_t.shape[0]
    m = h3.shape[0]
    smem = lambda width: pl.BlockSpec((1, 1, width), lambda s, j: (j, 0, 0), memory_space=pltpu.SMEM)
    stage = pltpu.VMEM((SUB_TOK * ROUND_SLOTS, SUBLANES, LANES), I32)
    tok_blk = pl.BlockSpec((TOK_TILE, 2 * SUBLANES, LANES), lambda s, j: (j, 0, 0))
    assert ntile >= 4, "in-place accumulation needs several token tiles between a tile's write and its next read"
    return pl.pallas_call(
        _peer_kernel,
        grid=(N_STRIPS, ntile),
        in_specs=[smem(ids_t.shape[2]), smem(TOK_TILE * (N_STRIPS + 1)),
                  pl.BlockSpec((TOK_TILE, N_SEL), lambda s, j: (j, 0)),
                  tok_blk,
                  pl.BlockSpec((1, STRIP * TABLE_ROWS, LANES), lambda s, j: (s, 0, 0), pipeline_mode=pl.Buffered(1)),
                  tok_blk],
        out_specs=tok_blk,
        out_shape=jax.ShapeDtypeStruct((m, 2 * SUBLANES, LANES), F32),
        input_output_aliases={5: 0},
        scratch_shapes=[stage, stage, stage, pltpu.SMEM((TOK_TILE // SUB_TOK,), I32)],
        compiler_params=_cparams(("arbitrary", "arbitrary"), VMEM_LIMIT),
        name="peer",
    )(ids_t, st_t, gs_t, h3, table, jnp.zeros((m, 2 * SUBLANES, LANES), F32))


def _final_kernel(x_ref, p_ref, gt_ref, g_ref, o_ref):
    x2 = x_ref[...] + gt_ref[0] * p_ref[...]
    o_ref[...] = x2 * lax.rsqrt(jnp.mean(x2 * x2, axis=-1, keepdims=True) + NORM_EPS) * g_ref[...]


def _final(x1, peer_out, gt, g, seq):
    m, d = x1.shape
    tm = min(512, seq)
    per_b = seq // tm
    return pl.pallas_call(
        _final_kernel,
        grid=(m // tm,),
        in_specs=[pl.BlockSpec((tm, d), lambda i: (i, 0)),
                  pl.BlockSpec((tm, d), lambda i: (i, 0)),
                  pl.BlockSpec((1, 1, d), lambda i: (i // per_b, 0, 0)),
                  pl.BlockSpec((1, d), lambda i: (0, 0))],
        out_specs=pl.BlockSpec((tm, d), lambda i: (i, 0)),
        out_shape=jax.ShapeDtypeStruct((m, d), F32),
        compiler_params=_cparams(("arbitrary",), VMEM_LIMIT),
        name="final",
    )(x1, peer_out, gt, g.reshape(1, d))


def _token_mixing(xf, seq, sh1, sc1, gt1, norm1_g, w_in, gmlp_v_g, gmlp_v_b, gmlp_ws, gmlp_bs, gmlp_out_g,
                  rwkv_mu, rwkv_w0, rwkv_w2, rwkv_a0, rwkv_a2, rwkv_g2, rwkv_kk, rwkv_ka, rwkv_rk,
                  rwkv_ln_g, rwkv_ln_b):
    w_pad = jnp.pad(w_in, ((0, 0), (0, IN_PAD - IN_WIDTH))).astype(BF16)
    p = _in_proj(xf, norm1_g, sh1, sc1, w_pad, seq)
    y_a = _gmlp(p, gmlp_v_g, gmlp_v_b, gmlp_ws, gmlp_bs, gmlp_out_g)
    r, k, v, kk, bvec, lw, g = _rwkv_prep(p, rwkv_mu, rwkv_w0, rwkv_w2, rwkv_a0, rwkv_a2, rwkv_g2,
                                          rwkv_kk, rwkv_ka, seq)
    y_b = _rwkv_scan(r, k, v, kk, bvec, lw, g, rwkv_rk, rwkv_ln_g, rwkv_ln_b, seq)
    return y_a, y_b


def _peer_ffn(h2, peer_wq, peer_keys, peer_u, peer_v):
    m, d = h2.shape
    keys_bf = peer_keys.reshape(2 * PEER_HEADS, PEER_KEYS, LANES).astype(BF16)
    scores = _q_scores(h2, peer_wq.astype(BF16), keys_bf)
    e3, g3 = _topk(scores.reshape(scores.shape[0], m // LANES, LANES))
    ids3, gs3, st3 = _route(e3, g3)
    table = _pack(_cast_bf16(peer_u), _cast_bf16(peer_v))
    return _peer(*_peer_operands(ids3, gs3, st3, h2, table)).reshape(m, d)


def _peer_operands(ids3, gs3, st3, h2, table):
    m, d = h2.shape
    ntile = m // TOK_TILE
    tok_major = lambda a_: a_.reshape(a_.shape[0], m).T
    ids_t = jnp.pad(tok_major(ids3).reshape(ntile, 1, TOK_TILE * N_SEL), ((0, 0), (0, 0), (0, IDS_PAD)))
    st_t = tok_major(st3).reshape(ntile, 1, TOK_TILE * (N_STRIPS + 1))
    return (ids_t, st_t, tok_major(gs3), h2.reshape(m, 2 * SUBLANES, LANES),
            table.reshape(N_STRIPS, STRIP * TABLE_ROWS, LANES))


def kernel(x, c, ada_w, ada_b, norm1_g, w_in, gmlp_v_g, gmlp_v_b, gmlp_ws, gmlp_bs, gmlp_out_g, rwkv_mu, rwkv_w0, rwkv_w2, rwkv_a0, rwkv_a2, rwkv_g2, rwkv_kk, rwkv_ka, rwkv_rk, rwkv_ln_g, rwkv_ln_b, w_out, norm2_g, peer_wq, peer_keys, peer_u, peer_v, final_g):
    bsz, seq, d = x.shape
    depth = ada_w.shape[0]
    xf = x.reshape(bsz * seq, d)
    for l in range(depth):
        mod = _mod(c, ada_w[l], ada_b[l])
        sh1, sc1, gt1, sh2, sc2, gt2 = [t.reshape(bsz, 1, d) for t in jnp.split(mod, 6, axis=-1)]
        y_a, y_b = _token_mixing(xf, seq, sh1, sc1, gt1, norm1_g[l], w_in[l], gmlp_v_g[l], gmlp_v_b[l],
                                 gmlp_ws[l], gmlp_bs[l], gmlp_out_g[l], rwkv_mu[l], rwkv_w0[l], rwkv_w2[l],
                                 rwkv_a0[l], rwkv_a2[l], rwkv_g2[l], rwkv_kk[l], rwkv_ka[l], rwkv_rk[l],
                                 rwkv_ln_g[l], rwkv_ln_b[l])
        x1, h2 = _out_proj(xf, y_a, y_b, w_out[l].astype(BF16), gt1, norm2_g[l], sh2, sc2, seq)
        parts = _peer_ffn(h2, peer_wq[l], peer_keys[l], peer_u[l], peer_v[l])
        last = l == depth - 1
        assert last, "multi-layer stacking needs an un-normalised residual output"
        xf = _final(x1, parts, gt2, final_g, seq)
    return xf.reshape(bsz, seq, d)
```

```python
import functools

import jax
import jax.numpy as jnp
from jax import lax
from jax.experimental import pallas as pl
from jax.experimental.pallas import tpu as pltpu

F32 = jnp.float32
BF16 = jnp.bfloat16
I32 = jnp.int32
HI = lax.Precision.HIGHEST

D_MODEL = 2048
GMLP_WIDTH = 1024
RWKV_WIDTH = 1024
GMLP_BLOCK = 128
GMLP_HEADS = 8
CHUNK = 64
HEAD_DIM = 64
DECAY_LORA = 64
AAA_LORA = 64
GATE_LORA = 160
LORA_PAD = 512
IN_WIDTH = 5408
IN_PAD = 5632
PEER_KEYS = 128
PEER_HEADS = 8
PEER_TOPK = 16
N_SEL = PEER_HEADS * PEER_TOPK
N_EXPERTS = PEER_KEYS * PEER_KEYS
N_STRIPS = 4
STRIP = N_EXPERTS // N_STRIPS
STRIP_SHIFT = STRIP.bit_length() - 1
TABLE_ROWS = 16
NORM_EPS = 1e-6
LN_EPS = 1e-5
GROUPNORM_EPS = 64e-5
SCAN_CHUNK = 64
LANES = 128
SUBLANES = 8
TOK_TILE = 64
SUB_TOK = 8
ROUND_SLOTS = 8
IDS_PAD = 384
VMEM_LIMIT = 56 * 1024 * 1024


def _cparams(sem, vmem=None):
    return pltpu.CompilerParams(dimension_semantics=sem, vmem_limit_bytes=vmem)


def _gelu(x):
    return 0.5 * x * (1.0 + lax.erf(x * 0.7071067811865476))


def _dot(a, b, **kw):
    return jnp.dot(a, b, preferred_element_type=F32, **kw)


def _dot_nt(a, b, **kw):
    return lax.dot_general(a, b, (((1,), (1,)), ((), ())), preferred_element_type=F32, **kw)


def _dot_tn(a, b, **kw):
    return lax.dot_general(a, b, (((0,), (0,)), ((), ())), preferred_element_type=F32, **kw)


def _mod_kernel(c_ref, w_ref, b_ref, o_ref):
    c = c_ref[...]
    sc = (c * jax.nn.sigmoid(c)).astype(BF16)
    o_ref[...] = _dot(sc, w_ref[...].astype(BF16)) + b_ref[...]


def _mod(c, w, b):
    bsz, d = c.shape
    n = w.shape[1]
    tn = 1024
    return pl.pallas_call(
        _mod_kernel,
        grid=(n // tn,),
        in_specs=[pl.BlockSpec((bsz, d), lambda j: (0, 0)),
                  pl.BlockSpec((d, tn), lambda j: (0, j)),
                  pl.BlockSpec((1, tn), lambda j: (0, j))],
        out_specs=pl.BlockSpec((bsz, tn), lambda j: (0, j)),
        out_shape=jax.ShapeDtypeStruct((bsz, n), F32),
        compiler_params=_cparams(("arbitrary",), VMEM_LIMIT),
        name="mod",
    )(c, w, b.reshape(1, n))


def _in_proj_kernel(x_ref, g_ref, sh_ref, sc_ref, w_ref, o_ref, h_scr):
    @pl.when(pl.program_id(1) == 0)
    def _():
        x = x_ref[...]
        y = x * lax.rsqrt(jnp.mean(x * x, axis=-1, keepdims=True) + NORM_EPS) * g_ref[...]
        h_scr[...] = (y * (1.0 + sc_ref[0]) + sh_ref[0]).astype(BF16)

    o_ref[...] = _dot(h_scr[...], w_ref[...])


def _in_proj(xf, g, sh, sc, w_bf, seq):
    m, d = xf.shape
    n = w_bf.shape[1]
    tm, tn = min(1024, seq), 512
    per_b = seq // tm
    return pl.pallas_call(
        _in_proj_kernel,
        grid=(m // tm, n // tn),
        in_specs=[pl.BlockSpec((tm, d), lambda i, j: (i, 0)),
                  pl.BlockSpec((1, d), lambda i, j: (0, 0)),
                  pl.BlockSpec((1, 1, d), lambda i, j: (i // per_b, 0, 0)),
                  pl.BlockSpec((1, 1, d), lambda i, j: (i // per_b, 0, 0)),
                  pl.BlockSpec((d, tn), lambda i, j: (0, j))],
        out_specs=pl.BlockSpec((tm, tn), lambda i, j: (i, j)),
        out_shape=jax.ShapeDtypeStruct((m, n), F32),
        scratch_shapes=[pltpu.VMEM((tm, d), BF16)],
        compiler_params=_cparams(("arbitrary", "arbitrary"), VMEM_LIMIT),
        name="in_proj",
    )(xf, g.reshape(1, d), sh, sc, w_bf)


def _gmlp_kernel(u_ref, v_ref, vg_ref, vb_ref, ws_ref, bs_ref, og_ref, o_ref):
    gu = _gelu(u_ref[...])
    gv = _gelu(v_ref[...])
    mu = jnp.mean(gv, axis=-1, keepdims=True)
    var = jnp.mean(jnp.square(gv - mu), axis=-1, keepdims=True)
    vn = ((gv - mu) * lax.rsqrt(var + LN_EPS) * vg_ref[...] + vb_ref[...]).astype(BF16)
    ri = lax.broadcasted_iota(I32, (GMLP_BLOCK, GMLP_BLOCK), 0) // CHUNK
    ci = lax.broadcasted_iota(I32, (GMLP_BLOCK, GMLP_BLOCK), 1) // CHUNK
    keep = ci <= ri
    bs = bs_ref[...]
    parts = []
    for h in range(GMLP_HEADS):
        w = jnp.where(keep, ws_ref[h], 0.0).astype(BF16)
        z = _dot(w, vn[:, h * LANES:(h + 1) * LANES]) + bs[:, h:h + 1]
        parts.append(gu[:, h * LANES:(h + 1) * LANES] * z)
    y = jnp.concatenate(parts, axis=-1)
    o_ref[...] = y * lax.rsqrt(jnp.mean(y * y, axis=-1, keepdims=True) + NORM_EPS) * og_ref[...]


def _gmlp(p, vg, vb, ws, bs, og):
    m = p.shape[0]
    w = GMLP_WIDTH
    return pl.pallas_call(
        _gmlp_kernel,
        grid=(m // GMLP_BLOCK,),
        in_specs=[pl.BlockSpec((GMLP_BLOCK, w), lambda i: (i, 0)),
                  pl.BlockSpec((GMLP_BLOCK, w), lambda i: (i, 1)),
                  pl.BlockSpec((1, w), lambda i: (0, 0)),
                  pl.BlockSpec((1, w), lambda i: (0, 0)),
                  pl.BlockSpec((GMLP_HEADS, GMLP_BLOCK, GMLP_BLOCK), lambda i: (0, 0, 0)),
                  pl.BlockSpec((GMLP_BLOCK, GMLP_HEADS), lambda i: (0, 0)),
                  pl.BlockSpec((1, w), lambda i: (0, 0))],
        out_specs=pl.BlockSpec((GMLP_BLOCK, w), lambda i: (i, 0)),
        out_shape=jax.ShapeDtypeStruct((m, w), F32),
        compiler_params=_cparams(("arbitrary",), VMEM_LIMIT),
        name="gmlp",
    )(p, p, vg.reshape(1, w), vb.reshape(1, w), ws, bs.T, og.reshape(1, w))


def _shift_mix(p, prev_row, mu):
    rolled = pltpu.roll(p, 1, 0)
    first = lax.broadcasted_iota(I32, p.shape, 0) == 0
    shifted = jnp.where(first, prev_row, rolled)
    return p + (shifted - p) * mu


def _rwkv_prep_kernel(r_ref, k_ref, v_ref, l_ref, mur_ref, muk_ref, muv_ref, mul_ref,
                      w0_ref, w2_ref, a0_ref, a2_ref, g2_ref, kkw_ref, ka_ref, hsum_ref,
                      ro_ref, ko_ref, vo_ref, kko_ref, bo_ref, lw_ref, go_ref,
                      pr_scr, pk_scr, pv_scr, plr_scr):
    @pl.when(pl.program_id(1) == 0)
    def _():
        pr_scr[...] = jnp.zeros_like(pr_scr)
        pk_scr[...] = jnp.zeros_like(pk_scr)
        pv_scr[...] = jnp.zeros_like(pv_scr)
        plr_scr[...] = jnp.zeros_like(plr_scr)

    rows = r_ref.shape[0]
    r_raw, k_raw, v_raw, l_raw = r_ref[...], k_ref[...], v_ref[...], l_ref[...]
    r = _shift_mix(r_raw, pr_scr[...], mur_ref[...])
    k = _shift_mix(k_raw, pk_scr[...], muk_ref[...])
    v = _shift_mix(v_raw, pv_scr[...], muv_ref[...])
    lo = _shift_mix(l_raw, plr_scr[...], mul_ref[...])
    pr_scr[...] = r_raw[rows - 1:rows, :]
    pk_scr[...] = k_raw[rows - 1:rows, :]
    pv_scr[...] = v_raw[rows - 1:rows, :]
    plr_scr[...] = l_raw[rows - 1:rows, :]

    wpre = w0_ref[...] + _dot(jnp.tanh(lo).astype(BF16), w2_ref[...])
    w = -jax.nn.softplus(-wpre) - 0.5
    a = jax.nn.sigmoid(a0_ref[...] + _dot(lo.astype(BF16), a2_ref[...]))
    g = _dot(jax.nn.sigmoid(lo).astype(BF16), g2_ref[...])

    kks = k * kkw_ref[...]
    sq = kks * kks
    hsum = hsum_ref[...]
    ss = jnp.concatenate([_dot(sq[:, j * LANES:(j + 1) * LANES], hsum, precision=HI)
                          for j in range(RWKV_WIDTH // LANES)], axis=-1)
    kk = kks / jnp.maximum(jnp.sqrt(ss), 1e-12)
    ro_ref[...] = r
    ko_ref[...] = k * (1.0 + (a - 1.0) * ka_ref[...])
    vo_ref[...] = v
    kko_ref[...] = kk
    bo_ref[...] = kk * a
    lw_ref[...] = -jnp.exp(w)
    go_ref[...] = g


def _rwkv_prep(p, mu, w0, w2, a0, a2, g2, kkw, ka, seq):
    m = p.shape[0]
    w = RWKV_WIDTH
    ts = min(256, seq)
    nb = seq // ts
    lora_cols = DECAY_LORA + AAA_LORA + GATE_LORA
    mu_r, mu_k, mu_v = mu[0:w], mu[w:2 * w], mu[2 * w:3 * w]
    mu_l = jnp.pad(mu[3 * w:], (0, LORA_PAD - lora_cols))
    lane_head = jnp.arange(LANES) // HEAD_DIM
    hsum = (lane_head[:, None] == lane_head[None, :]).astype(F32)
    pad_rows = lambda t, at: jnp.pad(t, ((at, LORA_PAD - at - t.shape[0]), (0, 0))).astype(BF16)
    w2 = pad_rows(w2, 0)
    a2 = pad_rows(a2, DECAY_LORA)
    g2 = pad_rows(g2, DECAY_LORA + AAA_LORA)
    row = lambda a_: a_.reshape(1, -1)
    big = pl.BlockSpec((ts, w), lambda b, i: (b * nb + i, 0))
    const = lambda shape: pl.BlockSpec(shape, lambda b, i: (0,) * len(shape))
    outs = pl.pallas_call(
        _rwkv_prep_kernel,
        grid=(m // seq, nb),
        in_specs=[pl.BlockSpec((ts, w), lambda b, i: (b * nb + i, 2)),
                  pl.BlockSpec((ts, w), lambda b, i: (b * nb + i, 3)),
                  pl.BlockSpec((ts, w), lambda b, i: (b * nb + i, 4)),
                  pl.BlockSpec((ts, LORA_PAD), lambda b, i: (b * nb + i, 10)),
                  const((1, w)), const((1, w)), const((1, w)), const((1, LORA_PAD)),
                  const((1, w)), const((LORA_PAD, w)), const((1, w)), const((LORA_PAD, w)),
                  const((LORA_PAD, w)), const((1, w)), const((1, w)), const((LANES, LANES))],
        out_specs=[big] * 7,
        out_shape=[jax.ShapeDtypeStruct((m, w), F32)] * 7,
        scratch_shapes=[pltpu.VMEM((1, w), F32), pltpu.VMEM((1, w), F32), pltpu.VMEM((1, w), F32),
                        pltpu.VMEM((1, LORA_PAD), F32)],
        compiler_params=_cparams(("arbitrary", "arbitrary"), VMEM_LIMIT),
        name="rwkv_prep",
    )(p, p, p, p, row(mu_r), row(mu_k), row(mu_v), row(mu_l),
      row(w0), w2, row(a0), a2, g2, row(kkw), row(ka), hsum)
    return outs


def _stack_heads(x):
    lane = lax.broadcasted_iota(I32, x.shape, 1)
    return jnp.concatenate([jnp.where(lane < HEAD_DIM, x, 0.0), jnp.where(lane >= HEAD_DIM, x, 0.0)], axis=0)


def _mm(a_list, b_list, dims, passes):
    dn = (dims, ((), ()))
    dg = lambda x, y, **kw: lax.dot_general(x, y, dn, preferred_element_type=F32, **kw)
    if passes == 6:
        return [dg(a, b, precision=HI) for a, b in zip(a_list, b_list)]
    ah = [a.astype(BF16) for a in a_list]
    bh = [b.astype(BF16) for b in b_list]
    main = [dg(x, y) for x, y in zip(ah, bh)]
    if passes == 1:
        return main
    al = [(a - h.astype(F32)).astype(BF16) for a, h in zip(a_list, ah)]
    bl = [(b - h.astype(F32)).astype(BF16) for b, h in zip(b_list, bh)]
    c1 = [dg(x, y) for x, y in zip(ah, bl)]
    c2 = [dg(x, y) for x, y in zip(al, bh)]
    return [m + (p + q) for m, p, q in zip(main, c1, c2)]


_NN = ((1,), (0,))
_NT = ((1,), (1,))
_TN = ((0,), (0,))
SCAN_PASSES = {"cumsum": 3, "scores": 1, "inverse": 3, "apply": 1, "state": 1, "chain": 3, "stats": 1}
SCAN_PAIRS = 8


def _unit_lower_inverse(a_list):
    n = a_list[0].shape[0]
    ri = lax.broadcasted_iota(I32, (n, n), 0)
    ci = lax.broadcasted_iota(I32, (n, n), 1)
    eye = (ri == ci).astype(F32)
    diag_blk = (ri // 16) == (ci // 16)
    mm = lambda xs, ys: _mm(xs, ys, _NN, SCAN_PASSES["inverse"])
    grow = lambda ts, ps: [t + p for t, p in zip(ts, ps)]
    d = [jnp.where(diag_blk, a, 0.0) for a in a_list]
    low = [a - x for a, x in zip(a_list, d)]
    d2 = mm(d, d)
    d4 = mm(d2, d2)
    d8 = mm(d4, d4)
    td = [eye + x for x in d]
    td = grow(td, mm(td, d2))
    td = grow(td, mm(td, d4))
    td = grow(td, mm(td, d8))
    nn = mm(td, low)
    n2 = mm(nn, nn)
    x = [eye + t for t in nn]
    x = grow(x, mm(x, n2))
    return mm(x, td)


def _scan_pairs(r, k, v, kk, bb, lw, g, rk, lng, lnb, s0):
    c = SCAN_CHUNK
    ps = SCAN_PASSES
    npair = len(r)
    ti = lax.broadcasted_iota(I32, (c, c), 0)
    si = lax.broadcasted_iota(I32, (c, c), 1)
    tril = (si <= ti).astype(F32)
    cum = _mm([tril] * npair, lw, _NN, ps["cumsum"])
    wcur = [jnp.exp(x) for x in cum]
    winv = [jnp.exp(-x) for x in cum]
    wprev = [jnp.exp(x - y) for x, y in zip(cum, lw)]
    rt = [_stack_heads(x * w) for x, w in zip(r, wcur)]
    kt = [_stack_heads(x * w) for x, w in zip(k, winv)]
    bt = [_stack_heads(x * w) for x, w in zip(bb, winv)]
    at = [_stack_heads(-x * w) for x, w in zip(kk, wprev)]
    vs = [_stack_heads(x) for x in v]

    n = 2 * c
    ri = lax.broadcasted_iota(I32, (n, n), 0)
    ci = lax.broadcasted_iota(I32, (n, n), 1)
    same = (ri // c) == (ci // c)
    strict = same & (ci < ri)
    incl = same & (ci <= ri)
    sc = _mm([jnp.concatenate([x, y], axis=0) for x, y in zip(at, rt)],
             [jnp.concatenate([x, y], axis=0) for x, y in zip(bt, kt)], _NT, ps["scores"])
    a_ab = [jnp.where(strict, x[0:n, 0:n], 0.0) for x in sc]
    a_ak = [jnp.where(strict, x[0:n, n:2 * n], 0.0) for x in sc]
    a_rb = [jnp.where(incl, x[n:2 * n, 0:n], 0.0) for x in sc]
    a_rk = [jnp.where(incl, x[n:2 * n, n:2 * n], 0.0) for x in sc]

    av = _mm(a_ak, vs, _NN, ps["apply"])
    ark_v = _mm(a_rk, vs, _NN, ps["apply"])
    vk = _mm(vs, kt, _TN, ps["state"])
    tinv = _unit_lower_inverse(a_ab)
    pq = _mm(tinv, [jnp.concatenate([x, y], axis=1) for x, y in zip(at, av)], _NN, ps["apply"])
    rb = _mm(a_rb, pq, _NN, ps["apply"])
    r2 = [x + y[:, 0:LANES] for x, y in zip(rt, rb)]
    y0 = [x[:, LANES:2 * LANES] + y for x, y in zip(rb, ark_v)]
    y = [x + z for x, z in zip(_mm(r2, s0, _NT, ps["chain"]), y0)]
    eye = (ri == ci).astype(F32)
    pqb = _mm(pq, bt, _TN, ps["state"])
    wc = [w[c - 1:c, :] for w in wcur]
    m_k = [(eye + x[0:LANES]) * w for x, w in zip(pqb, wc)]
    n_k = [(x[LANES:2 * LANES] + z) * w for x, z, w in zip(pqb, vk, wc)]
    s_new = [x + z for x, z in zip(_mm(s0, m_k, _NN, ps["chain"]), n_k)]

    yp = [x[0:c, :] + x[c:n, :] for x in y]
    lane = lax.broadcasted_iota(I32, (LANES, LANES), 0) // HEAD_DIM
    lane_t = lax.broadcasted_iota(I32, (LANES, LANES), 1) // HEAD_DIM
    hsum = [(lane == lane_t).astype(F32)] * npair
    inv_n = 1.0 / HEAD_DIM
    mu = [x * inv_n for x in _mm(yp, hsum, _NN, ps["stats"])]
    bonus = _mm([x * z * w for x, z, w in zip(r, k, rk)], hsum, _NN, ps["stats"])
    dev = [x - z for x, z in zip(yp, mu)]
    var = [x * inv_n for x in _mm([x * x for x in dev], hsum, _NN, ps["stats"])]
    out = [(dv * lax.rsqrt(vr + GROUPNORM_EPS) * lg + lb + bo * vv) * gg
           for dv, vr, lg, lb, bo, vv, gg in zip(dev, var, lng, lnb, bonus, v, g)]
    return out, s_new


def _rwkv_scan_kernel(r_ref, k_ref, v_ref, kk_ref, b_ref, lw_ref, g_ref, rk_ref, lng_ref, lnb_ref,
                      o_ref, s_scr):
    @pl.when(pl.program_id(2) == 0)
    def _():
        s_scr[...] = jnp.zeros_like(s_scr)

    sls = [slice(j * LANES, (j + 1) * LANES) for j in range(SCAN_PAIRS)]
    split = lambda ref: [ref[:, sl] for sl in sls]
    out, s_new = _scan_pairs(split(r_ref), split(k_ref), split(v_ref), split(kk_ref), split(b_ref), split(lw_ref),
                             split(g_ref), split(rk_ref), split(lng_ref), split(lnb_ref),
                             [s_scr[j] for j in range(SCAN_PAIRS)])
    for j in range(SCAN_PAIRS):
        o_ref[:, sls[j]] = out[j]
        s_scr[j] = s_new[j]


def _rwkv_scan(r, k, v, kk, bvec, lw, g, rk, lng, lnb, seq):
    m = r.shape[0]
    c = SCAN_CHUNK
    nc = seq // c
    width = SCAN_PAIRS * LANES
    blk = pl.BlockSpec((c, width), lambda b, h, i: (b * nc + i, h))
    prm = pl.BlockSpec((1, width), lambda b, h, i: (0, h))
    return pl.pallas_call(
        _rwkv_scan_kernel,
        grid=(m // seq, RWKV_WIDTH // width, nc),
        in_specs=[blk] * 7 + [prm] * 3,
        out_specs=blk,
        out_shape=jax.ShapeDtypeStruct((m, RWKV_WIDTH), F32),
        scratch_shapes=[pltpu.VMEM((SCAN_PAIRS, LANES, LANES), F32)],
        compiler_params=_cparams(("arbitrary", "arbitrary", "arbitrary"), VMEM_LIMIT),
        name="rwkv_scan",
    )(r, k, v, kk, bvec, lw, g, rk.reshape(1, -1), lng.reshape(1, -1), lnb.reshape(1, -1))


def _out_proj_kernel(x_ref, ya_ref, yb_ref, wa_ref, wb_ref, gt_ref, g2_ref, sh_ref, sc_ref, x1_ref, h_ref):
    y = _dot(ya_ref[...].astype(BF16), wa_ref[...]) + _dot(yb_ref[...].astype(BF16), wb_ref[...])
    x1 = x_ref[...] + gt_ref[0] * y
    x1_ref[...] = x1
    hn = x1 * lax.rsqrt(jnp.mean(x1 * x1, axis=-1, keepdims=True) + NORM_EPS) * g2_ref[...]
    h_ref[...] = hn * (1.0 + sc_ref[0]) + sh_ref[0]


def _out_proj(xf, ya, yb, w_bf, gt, g2, sh, sc, seq):
    m, d = xf.shape
    tm = min(256, seq)
    per_b = seq // tm
    half = GMLP_WIDTH
    bspec = pl.BlockSpec((1, 1, d), lambda i: (i // per_b, 0, 0))
    return pl.pallas_call(
        _out_proj_kernel,
        grid=(m // tm,),
        in_specs=[pl.BlockSpec((tm, d), lambda i: (i, 0)),
                  pl.BlockSpec((tm, half), lambda i: (i, 0)),
                  pl.BlockSpec((tm, half), lambda i: (i, 0)),
                  pl.BlockSpec((half, d), lambda i: (0, 0)),
                  pl.BlockSpec((half, d), lambda i: (1, 0)),
                  bspec,
                  pl.BlockSpec((1, d), lambda i: (0, 0)),
                  bspec, bspec],
        out_specs=[pl.BlockSpec((tm, d), lambda i: (i, 0)), pl.BlockSpec((tm, d), lambda i: (i, 0))],
        out_shape=[jax.ShapeDtypeStruct((m, d), F32), jax.ShapeDtypeStruct((m, d), F32)],
        compiler_params=_cparams(("arbitrary",), VMEM_LIMIT),
        name="out_proj",
    )(xf, ya, yb, w_bf, w_bf, gt, g2.reshape(1, d), sh, sc)


def _q_scores_kernel(h_ref, wq_ref, keys_ref, o_ref):
    q = _dot(h_ref[...].astype(BF16), wq_ref[...]).astype(BF16)
    for hp in range(2 * PEER_HEADS):
        o_ref[hp * PEER_KEYS:(hp + 1) * PEER_KEYS, :] = _dot_nt(
            keys_ref[hp], q[:, hp * LANES:(hp + 1) * LANES])


def _q_scores(h, wq_bf, keys_bf):
    m, d = h.shape
    tm = 256
    n = wq_bf.shape[1]
    return pl.pallas_call(
        _q_scores_kernel,
        grid=(m // tm,),
        in_specs=[pl.BlockSpec((tm, d), lambda i: (i, 0)),
                  pl.BlockSpec((d, n), lambda i: (0, 0)),
                  pl.BlockSpec((2 * PEER_HEADS, PEER_KEYS, LANES), lambda i: (0, 0, 0))],
        out_specs=pl.BlockSpec((n, tm), lambda i: (0, i)),
        out_shape=jax.ShapeDtypeStruct((n, m), F32),
        compiler_params=_cparams(("arbitrary",), VMEM_LIMIT),
        name="q_scores",
    )(h, wq_bf, keys_bf)


def _extract_topk(vals, k):
    rows = vals.shape[0]
    iota = lax.broadcasted_iota(I32, vals.shape, 0)
    tops, poss = [], []
    for _ in range(k):
        m = jnp.max(vals, axis=0)
        pos = jnp.min(jnp.where(vals == m[None], iota, rows), axis=0)
        vals = jnp.where(iota == pos[None], -jnp.inf, vals)
        tops.append(m)
        poss.append(pos)
    return tops, poss


def _topk_kernel(s_ref, e_ref, g_ref, st_scr, it_scr):
    k = PEER_TOPK

    def half_body(hp, carry):
        vals = s_ref[pl.ds(pl.multiple_of(hp * PEER_KEYS, PEER_KEYS), PEER_KEYS)]
        tops, poss = _extract_topk(vals, k)
        for j in range(k):
            st_scr[hp, j] = tops[j]
            it_scr[hp, j] = poss[j]
        return carry

    lax.fori_loop(0, 2 * PEER_HEADS, half_body, 0)

    def head_body(h, carry):
        s1 = [st_scr[2 * h, j] for j in range(k)]
        s2 = [st_scr[2 * h + 1, j] for j in range(k)]
        i1 = [it_scr[2 * h, j] for j in range(k)]
        i2 = [it_scr[2 * h + 1, j] for j in range(k)]
        pairs = [(a, b) for a in range(k) for b in range(k) if (a + 1) * (b + 1) <= k]
        cand = jnp.stack([s1[a] + s2[b] for a, b in pairs], axis=0)
        tops, poss = _extract_topk(cand, k)
        exps = [jnp.exp(t - tops[0]) for t in tops]
        den = exps[0]
        for t in exps[1:]:
            den = den + t
        for j in range(k):
            flat = jnp.zeros_like(poss[j])
            for r, (a, b) in enumerate(pairs):
                flat = jnp.where(poss[j] == r, a * k + b, flat)
            pa = flat >> 4
            pb = flat & (k - 1)
            ia = jnp.zeros_like(pa)
            ib = jnp.zeros_like(pa)
            for a in range(k):
                ia = jnp.where(pa == a, i1[a], ia)
                ib = jnp.where(pb == a, i2[a], ib)
            e_ref[h * k + j] = ia * PEER_KEYS + ib
            g_ref[h * k + j] = exps[j] / den
        return carry

    lax.fori_loop(0, PEER_HEADS, head_body, 0)


def _topk(scores3):
    n, nb, _ = scores3.shape
    blk = pl.BlockSpec((N_SEL, SUBLANES, LANES), lambda i: (0, i, 0))
    return pl.pallas_call(
        _topk_kernel,
        grid=(nb // SUBLANES,),
        in_specs=[pl.BlockSpec((n, SUBLANES, LANES), lambda i: (0, i, 0))],
        out_specs=[blk, blk],
        out_shape=[jax.ShapeDtypeStruct((N_SEL, nb, LANES), I32), jax.ShapeDtypeStruct((N_SEL, nb, LANES), F32)],
        scratch_shapes=[pltpu.VMEM((2 * PEER_HEADS, PEER_TOPK, SUBLANES, LANES), F32),
                        pltpu.VMEM((2 * PEER_HEADS, PEER_TOPK, SUBLANES, LANES), I32)],
        compiler_params=_cparams(("arbitrary",), VMEM_LIMIT),
        name="topk",
    )(scores3)


def _route_kernel(e_ref, g_ref, ids_ref, gs_ref, st_ref, q_scr):
    zero = jnp.zeros((SUBLANES, LANES), I32)
    counts = [zero] * N_STRIPS
    for kk in range(N_SEL):
        strip = e_ref[kk] >> STRIP_SHIFT
        rank = zero
        for s in range(N_STRIPS):
            hit = strip == s
            rank = jnp.where(hit, counts[s], rank)
            counts[s] = counts[s] + hit.astype(I32)
        q_scr[kk] = rank
    start = zero
    starts = []
    for s in range(N_STRIPS):
        starts.append(start)
        st_ref[s] = start
        start = start + counts[s]
    st_ref[N_STRIPS] = start

    def fix(kk, carry):
        e = e_ref[kk]
        strip = e >> STRIP_SHIFT
        base = zero
        for s in range(N_STRIPS):
            base = jnp.where(strip == s, starts[s], base)
        q_scr[kk] = q_scr[kk] + base
        return carry

    lax.fori_loop(0, N_SEL, fix, 0)

    def place(q, carry):
        acc_i = zero
        acc_g = jnp.zeros((SUBLANES, LANES), F32)
        for kk in range(N_SEL):
            hit = q_scr[kk] == q
            acc_i = jnp.where(hit, e_ref[kk], acc_i)
            acc_g = jnp.where(hit, g_ref[kk], acc_g)
        ids_ref[q] = (acc_i & (STRIP - 1)) * TABLE_ROWS
        gs_ref[q] = acc_g
        return carry

    lax.fori_loop(0, N_SEL, place, 0)


def _route(e3, g3):
    _, nb, _ = e3.shape
    blk = pl.BlockSpec((N_SEL, SUBLANES, LANES), lambda i: (0, i, 0))
    sblk = pl.BlockSpec((N_STRIPS + 1, SUBLANES, LANES), lambda i: (0, i, 0))
    return pl.pallas_call(
        _route_kernel,
        grid=(nb // SUBLANES,),
        in_specs=[blk, blk],
        out_specs=[blk, blk, sblk],
        out_shape=[jax.ShapeDtypeStruct((N_SEL, nb, LANES), I32), jax.ShapeDtypeStruct((N_SEL, nb, LANES), F32),
                   jax.ShapeDtypeStruct((N_STRIPS + 1, nb, LANES), I32)],
        scratch_shapes=[pltpu.VMEM((N_SEL, SUBLANES, LANES), I32)],
        compiler_params=_cparams(("arbitrary",), VMEM_LIMIT),
        name="route",
    )(e3, g3)


def _cast_kernel(t_ref, o_ref):
    o_ref[...] = t_ref[...].astype(BF16)


def _cast_bf16(table):
    n, d = table.shape
    tr = 512
    return pl.pallas_call(
        _cast_kernel,
        grid=(n // tr,),
        in_specs=[pl.BlockSpec((tr, d), lambda i: (i, 0))],
        out_specs=pl.BlockSpec((tr, d), lambda i: (i, 0)),
        out_shape=jax.ShapeDtypeStruct((n, d), BF16),
        compiler_params=_cparams(("arbitrary",), VMEM_LIMIT),
        name="cast",
    )(table)


def _pack_kernel(u_ref, v_ref, o_ref):
    tr = u_ref.shape[0]
    words = lambda ref: pltpu.bitcast(ref[...].reshape(tr * 2 * SUBLANES, LANES), I32).reshape(tr, SUBLANES, LANES)
    o_ref[:, 0] = words(u_ref)
    o_ref[:, 1] = words(v_ref)


def _pack(u_bf, v_bf):
    n, d = u_bf.shape
    tr = 256
    tile = lambda t: t.reshape(n, d // LANES, LANES)
    blk = pl.BlockSpec((tr, d // LANES, LANES), lambda i: (i, 0, 0))
    return pl.pallas_call(
        _pack_kernel,
        grid=(n // tr,),
        in_specs=[blk, blk],
        out_specs=pl.BlockSpec((tr, 2, SUBLANES, LANES), lambda i: (i, 0, 0, 0)),
        out_shape=jax.ShapeDtypeStruct((n, 2, SUBLANES, LANES), I32),
        compiler_params=_cparams(("arbitrary",), VMEM_LIMIT),
        name="pack",
    )(tile(u_bf), tile(v_bf))


def _fold8(xs):
    sub = lax.broadcasted_iota(I32, (SUBLANES, LANES), 0)
    xs = [xs[t ^ 3] for t in range(8)]
    low = sub < 4
    t4 = []
    for i in range(4):
        keep = jnp.where(low, xs[i], xs[i + 4])
        move = jnp.where(low, xs[i + 4], xs[i])
        t4.append(keep + pltpu.roll(move, 4, 0))
    m2 = (sub & 2) != 0
    t2 = [jnp.where(m2, t4[i] + pltpu.roll(t4[i], 2, 0), t4[i + 2] + pltpu.roll(t4[i + 2], 6, 0)) for i in range(2)]
    m1 = (sub & 1) != 0
    return jnp.where(m1, t2[0] + pltpu.roll(t2[0], 1, 0), t2[1] + pltpu.roll(t2[1], 7, 0))


def _split_dot(x, w_bf):
    rows = x.shape[0]
    hi = x.astype(BF16)
    lo = (x - hi.astype(F32)).astype(BF16)
    both = _dot(jnp.concatenate([hi, lo], axis=0), w_bf)
    return both[0:rows] + both[rows:2 * rows]


def _peer_prepare(s, st_ref, gs_ref, gsr_ref, cnt_ref, idb_ref, nr_ref):
    lane = lax.broadcasted_iota(I32, (SUBLANES, LANES), 1)
    subl = lax.broadcasted_iota(I32, (SUBLANES, LANES), 0)
    nst = N_STRIPS + 1
    total = jnp.int32(0)
    for sub in range(TOK_TILE // SUB_TOK):
        cmax = jnp.int32(0)
        rolled = jnp.zeros((SUBLANES, LANES), F32)
        counts = jnp.zeros((SUBLANES, LANES), I32)
        gs8 = gs_ref[pl.ds(sub * SUB_TOK, SUB_TOK), :]
        for t in range(SUB_TOK):
            tok = sub * SUB_TOK + t
            beg = st_ref[0, 0, tok * nst + s]
            cnt = st_ref[0, 0, tok * nst + s + 1] - beg
            idb_ref[tok] = tok * N_SEL + beg
            cmax = jnp.maximum(cmax, cnt)
            row = pltpu.roll(gs8, (N_SEL - beg) & (N_SEL - 1), 1)
            rolled = jnp.where(subl == t, row, rolled)
            counts = jnp.where(subl == t, cnt, counts)
        gsr_ref[pl.ds(sub * SUB_TOK, SUB_TOK), :] = rolled
        cnt_ref[pl.ds(sub * SUB_TOK, SUB_TOK), :] = counts
        nr = jnp.maximum((cmax + ROUND_SLOTS - 1) // ROUND_SLOTS, 1)
        nr_ref[sub] = nr
        total = total + nr
    return total


def _peer_gather(sub, rd, ids_ref, gsr_ref, cnt_ref, idb_ref, h_ref, tab_ref, stage_ref):
    lane = lax.broadcasted_iota(I32, (SUBLANES, LANES), 1)
    first = rd * ROUND_SLOTS
    tok0 = pl.multiple_of(sub * SUB_TOK, SUB_TOK)
    gates = pltpu.roll(gsr_ref[pl.ds(tok0, SUB_TOK), :], (N_SEL - first) & (N_SEL - 1), 1)
    g8 = jnp.where(lane < cnt_ref[pl.ds(tok0, SUB_TOK), :] - first, gates, 0.0)
    h8 = h_ref.at[pl.ds(sub * SUB_TOK, SUB_TOK)]
    ids0 = [idb_ref[sub * SUB_TOK + t] + first for t in range(SUB_TOK)]
    h = [h8[t] for t in range(SUB_TOK)]
    red = []
    for j in range(ROUND_SLOTS):
        parts = []
        for t in range(SUB_TOK):
            row = pl.multiple_of(ids_ref[0, 0, ids0[t] + j], TABLE_ROWS)
            u = pltpu.bitcast(tab_ref[0, pl.ds(row, SUBLANES), :], BF16)
            p = u.astype(F32) * h[t]
            parts.append(p[0:SUBLANES] + p[SUBLANES:2 * SUBLANES])
            stage_ref[t * ROUND_SLOTS + j] = tab_ref[0, pl.ds(row + SUBLANES, SUBLANES), :]
        red.append(_fold8(parts))
    return jnp.concatenate(red, axis=0), g8


def _peer_activate(red, g8):
    lane = lax.broadcasted_iota(I32, (SUBLANES, LANES), 1)
    group = lane // (LANES // ROUND_SLOTS)
    pre = jnp.zeros((SUBLANES, LANES), F32)
    for j in range(ROUND_SLOTS):
        dots = jnp.sum(red[j * SUBLANES:(j + 1) * SUBLANES], axis=-1, keepdims=True)
        pre = jnp.where(group == j, dots, pre)
    ri = lax.broadcasted_iota(I32, (LANES, LANES), 0)
    ci = lax.broadcasted_iota(I32, (LANES, LANES), 1)
    spread = (ri == ci // (LANES // ROUND_SLOTS)).astype(BF16)
    return _gelu(pre) * _split_dot(g8, spread)


def _peer_apply(psub, act, stage_ref, o_ref):
    row16 = lax.broadcasted_iota(I32, (2 * SUBLANES, LANES), 0)
    lane16 = lax.broadcasted_iota(I32, (2 * SUBLANES, LANES), 1)
    diag = (lane16 & (2 * SUBLANES - 1)) == row16
    o8 = o_ref.at[pl.ds(psub * SUB_TOK, SUB_TOK)]
    for t in range(SUB_TOK):
        a = jnp.where(diag, jnp.broadcast_to(act[t:t + 1, :], (2 * SUBLANES, LANES)), 0.0).astype(BF16)
        w = pltpu.bitcast(stage_ref[pl.ds(t * ROUND_SLOTS, ROUND_SLOTS)].reshape(ROUND_SLOTS * SUBLANES, LANES), BF16)
        o8[t] = o8[t] + _dot(a, w)


def _peer_kernel(ids_ref, st_ref, gs_ref, h_ref, tab_ref, acc_ref, o_ref, stage_a, stage_b, stage_c, gsr_ref, cnt_ref,
                 idb_ref, nr_ref):
    s = pl.program_id(0)
    nsub = TOK_TILE // SUB_TOK
    o_ref[...] = acc_ref[...]
    total = _peer_prepare(s, st_ref, gs_ref, gsr_ref, cnt_ref, idb_ref, nr_ref)
    done_rd = N_SEL // ROUND_SLOTS

    def step(state, stage_new, stage_old):
        sub, rd, sub_mid, sub_old, red, g8, act_old = state
        act = _peer_activate(red, g8)
        _peer_apply(sub_old, act_old, stage_old, o_ref)
        live = sub < nsub
        sub_c = jnp.minimum(sub, nsub - 1)
        red, g8 = _peer_gather(sub_c, jnp.where(live, rd, done_rd), ids_ref, gsr_ref, cnt_ref, idb_ref, h_ref, tab_ref,
                               stage_new)
        last = rd + 1 >= nr_ref[sub_c]
        return (jnp.where(last, sub + 1, sub), jnp.where(last, 0, rd + 1), sub_c, sub_mid, red, g8, act)

    def body(i, state):
        state = step(state, stage_a, stage_b)
        state = step(state, stage_b, stage_c)
        return step(state, stage_c, stage_a)

    stage_b[...] = jnp.zeros_like(stage_b)
    stage_c[...] = jnp.zeros_like(stage_c)
    zero = jnp.int32(0)
    init = (zero, zero, zero, zero, jnp.zeros((SUB_TOK * ROUND_SLOTS, LANES), F32), jnp.zeros((SUBLANES, LANES), F32),
            jnp.zeros((SUBLANES, LANES), F32))
    lax.fori_loop(0, (total + 4) // 3, body, init)


def _peer(ids_t, st_t, gs_t, h3, table):
    ntile = ids_t.shape[0]
    m = h3.shape[0]
    smem = lambda width: pl.BlockSpec((1, 1, width), lambda s, j: (j, 0, 0), memory_space=pltpu.SMEM)
    stage = pltpu.VMEM((SUB_TOK * ROUND_SLOTS, SUBLANES, LANES), I32)
    tok_blk = pl.BlockSpec((TOK_TILE, 2 * SUBLANES, LANES), lambda s, j: (j, 0, 0))
    assert ntile >= 4, "in-place accumulation needs several token tiles between a tile's write and its next read"
    return pl.pallas_call(
        _peer_kernel,
        grid=(N_STRIPS, ntile),
        in_specs=[smem(ids_t.shape[2]), smem(TOK_TILE * (N_STRIPS + 1)),
                  pl.BlockSpec((TOK_TILE, N_SEL), lambda s, j: (j, 0)),
                  tok_blk,
                  pl.BlockSpec((1, STRIP * TABLE_ROWS, LANES), lambda s, j: (s, 0, 0), pipeline_mode=pl.Buffered(1)),
                  tok_blk],
        out_specs=tok_blk,
        out_shape=jax.ShapeDtypeStruct((m, 2 * SUBLANES, LANES), F32),
        input_output_aliases={5: 0},
        scratch_shapes=[stage, stage, stage, pltpu.VMEM((TOK_TILE, LANES), F32), pltpu.VMEM((TOK_TILE, LANES), I32),
                        pltpu.SMEM((TOK_TILE,), I32), pltpu.SMEM((TOK_TILE // SUB_TOK,), I32)],
        compiler_params=_cparams(("arbitrary", "arbitrary"), VMEM_LIMIT),
        name="peer",
    )(ids_t, st_t, gs_t, h3, table, jnp.zeros((m, 2 * SUBLANES, LANES), F32))


def _final_kernel(x_ref, p_ref, gt_ref, g_ref, o_ref):
    x2 = x_ref[...] + gt_ref[0] * p_ref[...]
    o_ref[...] = x2 * lax.rsqrt(jnp.mean(x2 * x2, axis=-1, keepdims=True) + NORM_EPS) * g_ref[...]


def _final(x1, peer_out, gt, g, seq):
    m, d = x1.shape
    tm = min(512, seq)
    per_b = seq // tm
    return pl.pallas_call(
        _final_kernel,
        grid=(m // tm,),
        in_specs=[pl.BlockSpec((tm, d), lambda i: (i, 0)),
                  pl.BlockSpec((tm, d), lambda i: (i, 0)),
                  pl.BlockSpec((1, 1, d), lambda i: (i // per_b, 0, 0)),
                  pl.BlockSpec((1, d), lambda i: (0, 0))],
        out_specs=pl.BlockSpec((tm, d), lambda i: (i, 0)),
        out_shape=jax.ShapeDtypeStruct((m, d), F32),
        compiler_params=_cparams(("arbitrary",), VMEM_LIMIT),
        name="final",
    )(x1, peer_out, gt, g.reshape(1, d))


def _token_mixing(xf, seq, sh1, sc1, gt1, norm1_g, w_in, gmlp_v_g, gmlp_v_b, gmlp_ws, gmlp_bs, gmlp_out_g,
                  rwkv_mu, rwkv_w0, rwkv_w2, rwkv_a0, rwkv_a2, rwkv_g2, rwkv_kk, rwkv_ka, rwkv_rk,
                  rwkv_ln_g, rwkv_ln_b):
    w_pad = jnp.pad(w_in, ((0, 0), (0, IN_PAD - IN_WIDTH))).astype(BF16)
    p = _in_proj(xf, norm1_g, sh1, sc1, w_pad, seq)
    y_a = _gmlp(p, gmlp_v_g, gmlp_v_b, gmlp_ws, gmlp_bs, gmlp_out_g)
    r, k, v, kk, bvec, lw, g = _rwkv_prep(p, rwkv_mu, rwkv_w0, rwkv_w2, rwkv_a0, rwkv_a2, rwkv_g2,
                                          rwkv_kk, rwkv_ka, seq)
    y_b = _rwkv_scan(r, k, v, kk, bvec, lw, g, rwkv_rk, rwkv_ln_g, rwkv_ln_b, seq)
    return y_a, y_b


def _peer_ffn(h2, peer_wq, peer_keys, peer_u, peer_v):
    m, d = h2.shape
    keys_bf = peer_keys.reshape(2 * PEER_HEADS, PEER_KEYS, LANES).astype(BF16)
    scores = _q_scores(h2, peer_wq.astype(BF16), keys_bf)
    e3, g3 = _topk(scores.reshape(scores.shape[0], m // LANES, LANES))
    ids3, gs3, st3 = _route(e3, g3)
    table = _pack(_cast_bf16(peer_u), _cast_bf16(peer_v))
    return _peer(*_peer_operands(ids3, gs3, st3, h2, table)).reshape(m, d)


def _peer_operands(ids3, gs3, st3, h2, table):
    m, d = h2.shape
    ntile = m // TOK_TILE
    tok_major = lambda a_: a_.reshape(a_.shape[0], m).T
    ids_t = jnp.pad(tok_major(ids3).reshape(ntile, 1, TOK_TILE * N_SEL), ((0, 0), (0, 0), (0, IDS_PAD)))
    st_t = tok_major(st3).reshape(ntile, 1, TOK_TILE * (N_STRIPS + 1))
    return (ids_t, st_t, tok_major(gs3), h2.reshape(m, 2 * SUBLANES, LANES),
            table.reshape(N_STRIPS, STRIP * TABLE_ROWS, LANES))


def kernel(x, c, ada_w, ada_b, norm1_g, w_in, gmlp_v_g, gmlp_v_b, gmlp_ws, gmlp_bs, gmlp_out_g, rwkv_mu, rwkv_w0, rwkv_w2, rwkv_a0, rwkv_a2, rwkv_g2, rwkv_kk, rwkv_ka, rwkv_rk, rwkv_ln_g, rwkv_ln_b, w_out, norm2_g, peer_wq, peer_keys, peer_u, peer_v, final_g):
    bsz, seq, d = x.shape
    depth = ada_w.shape[0]
    xf = x.reshape(bsz * seq, d)
    for l in range(depth):
        mod = _mod(c, ada_w[l], ada_b[l])
        sh1, sc1, gt1, sh2, sc2, gt2 = [t.reshape(bsz, 1, d) for t in jnp.split(mod, 6, axis=-1)]
        y_a, y_b = _token_mixing(xf, seq, sh1, sc1, gt1, norm1_g[l], w_in[l], gmlp_v_g[l], gmlp_v_b[l],
                                 gmlp_ws[l], gmlp_bs[l], gmlp_out_g[l], rwkv_mu[l], rwkv_w0[l], rwkv_w2[l],
                                 rwkv_a0[l], rwkv_a2[l], rwkv_g2[l], rwkv_kk[l], rwkv_ka[l], rwkv_rk[l],
                                 rwkv_ln_g[l], rwkv_ln_b[l])
        x1, h2 = _out_proj(xf, y_a, y_b, w_out[l].astype(BF16), gt1, norm2_g[l], sh2, sc2, seq)
        parts = _peer_ffn(h2, peer_wq[l], peer_keys[l], peer_u[l], peer_v[l])
        last = l == depth - 1
        assert last, "multi-layer stacking needs an un-normalised residual output"
        xf = _final(x1, parts, gt2, final_g, seq)
    return xf.reshape(bsz, seq, d)
```

```python
import functools

import jax
import jax.numpy as jnp
from jax import lax
from jax.experimental import pallas as pl
from jax.experimental.pallas import tpu as pltpu

F32 = jnp.float32
BF16 = jnp.bfloat16
I32 = jnp.int32
HI = lax.Precision.HIGHEST

D_MODEL = 2048
GMLP_WIDTH = 1024
RWKV_WIDTH = 1024
GMLP_BLOCK = 128
GMLP_HEADS = 8
CHUNK = 64
HEAD_DIM = 64
DECAY_LORA = 64
AAA_LORA = 64
GATE_LORA = 160
LORA_PAD = 512
IN_WIDTH = 5408
IN_PAD = 5632
PEER_KEYS = 128
PEER_HEADS = 8
PEER_TOPK = 16
N_SEL = PEER_HEADS * PEER_TOPK
N_EXPERTS = PEER_KEYS * PEER_KEYS
N_STRIPS = 4
STRIP = N_EXPERTS // N_STRIPS
STRIP_SHIFT = STRIP.bit_length() - 1
TABLE_ROWS = 16
NORM_EPS = 1e-6
LN_EPS = 1e-5
GROUPNORM_EPS = 64e-5
SCAN_CHUNK = 64
LANES = 128
SUBLANES = 8
TOK_TILE = 256
TRIP_STEPS = 12
SUB_TOK = 8
ROUND_SLOTS = 8
IDS_PAD = 384
VMEM_LIMIT = 56 * 1024 * 1024


def _cparams(sem, vmem=None):
    return pltpu.CompilerParams(dimension_semantics=sem, vmem_limit_bytes=vmem)


def _gelu(x):
    return 0.5 * x * (1.0 + lax.erf(x * 0.7071067811865476))


def _dot(a, b, **kw):
    return jnp.dot(a, b, preferred_element_type=F32, **kw)


def _dot_nt(a, b, **kw):
    return lax.dot_general(a, b, (((1,), (1,)), ((), ())), preferred_element_type=F32, **kw)


def _dot_tn(a, b, **kw):
    return lax.dot_general(a, b, (((0,), (0,)), ((), ())), preferred_element_type=F32, **kw)


def _mod_kernel(c_ref, w_ref, b_ref, o_ref):
    c = c_ref[...]
    sc = (c * jax.nn.sigmoid(c)).astype(BF16)
    o_ref[...] = _dot(sc, w_ref[...].astype(BF16)) + b_ref[...]


def _mod(c, w, b):
    bsz, d = c.shape
    n = w.shape[1]
    tn = 1024
    return pl.pallas_call(
        _mod_kernel,
        grid=(n // tn,),
        in_specs=[pl.BlockSpec((bsz, d), lambda j: (0, 0)),
                  pl.BlockSpec((d, tn), lambda j: (0, j)),
                  pl.BlockSpec((1, tn), lambda j: (0, j))],
        out_specs=pl.BlockSpec((bsz, tn), lambda j: (0, j)),
        out_shape=jax.ShapeDtypeStruct((bsz, n), F32),
        compiler_params=_cparams(("arbitrary",), VMEM_LIMIT),
        name="mod",
    )(c, w, b.reshape(1, n))


def _in_proj_kernel(x_ref, g_ref, sh_ref, sc_ref, w_ref, o_ref, h_scr):
    @pl.when(pl.program_id(1) == 0)
    def _():
        x = x_ref[...]
        y = x * lax.rsqrt(jnp.mean(x * x, axis=-1, keepdims=True) + NORM_EPS) * g_ref[...]
        h_scr[...] = (y * (1.0 + sc_ref[0]) + sh_ref[0]).astype(BF16)

    o_ref[...] = _dot(h_scr[...], w_ref[...])


def _in_proj(xf, g, sh, sc, w_bf, seq):
    m, d = xf.shape
    n = w_bf.shape[1]
    tm, tn = min(1024, seq), 512
    per_b = seq // tm
    return pl.pallas_call(
        _in_proj_kernel,
        grid=(m // tm, n // tn),
        in_specs=[pl.BlockSpec((tm, d), lambda i, j: (i, 0)),
                  pl.BlockSpec((1, d), lambda i, j: (0, 0)),
                  pl.BlockSpec((1, 1, d), lambda i, j: (i // per_b, 0, 0)),
                  pl.BlockSpec((1, 1, d), lambda i, j: (i // per_b, 0, 0)),
                  pl.BlockSpec((d, tn), lambda i, j: (0, j))],
        out_specs=pl.BlockSpec((tm, tn), lambda i, j: (i, j)),
        out_shape=jax.ShapeDtypeStruct((m, n), F32),
        scratch_shapes=[pltpu.VMEM((tm, d), BF16)],
        compiler_params=_cparams(("arbitrary", "arbitrary"), VMEM_LIMIT),
        name="in_proj",
    )(xf, g.reshape(1, d), sh, sc, w_bf)


def _gmlp_kernel(u_ref, v_ref, vg_ref, vb_ref, ws_ref, bs_ref, og_ref, o_ref):
    gu = _gelu(u_ref[...])
    gv = _gelu(v_ref[...])
    mu = jnp.mean(gv, axis=-1, keepdims=True)
    var = jnp.mean(jnp.square(gv - mu), axis=-1, keepdims=True)
    vn = ((gv - mu) * lax.rsqrt(var + LN_EPS) * vg_ref[...] + vb_ref[...]).astype(BF16)
    ri = lax.broadcasted_iota(I32, (GMLP_BLOCK, GMLP_BLOCK), 0) // CHUNK
    ci = lax.broadcasted_iota(I32, (GMLP_BLOCK, GMLP_BLOCK), 1) // CHUNK
    keep = ci <= ri
    bs = bs_ref[...]
    parts = []
    for h in range(GMLP_HEADS):
        w = jnp.where(keep, ws_ref[h], 0.0).astype(BF16)
        z = _dot(w, vn[:, h * LANES:(h + 1) * LANES]) + bs[:, h:h + 1]
        parts.append(gu[:, h * LANES:(h + 1) * LANES] * z)
    y = jnp.concatenate(parts, axis=-1)
    o_ref[...] = y * lax.rsqrt(jnp.mean(y * y, axis=-1, keepdims=True) + NORM_EPS) * og_ref[...]


def _gmlp(p, vg, vb, ws, bs, og):
    m = p.shape[0]
    w = GMLP_WIDTH
    return pl.pallas_call(
        _gmlp_kernel,
        grid=(m // GMLP_BLOCK,),
        in_specs=[pl.BlockSpec((GMLP_BLOCK, w), lambda i: (i, 0)),
                  pl.BlockSpec((GMLP_BLOCK, w), lambda i: (i, 1)),
                  pl.BlockSpec((1, w), lambda i: (0, 0)),
                  pl.BlockSpec((1, w), lambda i: (0, 0)),
                  pl.BlockSpec((GMLP_HEADS, GMLP_BLOCK, GMLP_BLOCK), lambda i: (0, 0, 0)),
                  pl.BlockSpec((GMLP_BLOCK, GMLP_HEADS), lambda i: (0, 0)),
                  pl.BlockSpec((1, w), lambda i: (0, 0))],
        out_specs=pl.BlockSpec((GMLP_BLOCK, w), lambda i: (i, 0)),
        out_shape=jax.ShapeDtypeStruct((m, w), F32),
        compiler_params=_cparams(("arbitrary",), VMEM_LIMIT),
        name="gmlp",
    )(p, p, vg.reshape(1, w), vb.reshape(1, w), ws, bs.T, og.reshape(1, w))


def _shift_mix(p, prev_row, mu):
    rolled = pltpu.roll(p, 1, 0)
    first = lax.broadcasted_iota(I32, p.shape, 0) == 0
    shifted = jnp.where(first, prev_row, rolled)
    return p + (shifted - p) * mu


def _rwkv_prep_kernel(r_ref, k_ref, v_ref, l_ref, mur_ref, muk_ref, muv_ref, mul_ref,
                      w0_ref, w2_ref, a0_ref, a2_ref, g2_ref, kkw_ref, ka_ref, hsum_ref,
                      ro_ref, ko_ref, vo_ref, kko_ref, bo_ref, lw_ref, go_ref,
                      pr_scr, pk_scr, pv_scr, plr_scr):
    @pl.when(pl.program_id(1) == 0)
    def _():
        pr_scr[...] = jnp.zeros_like(pr_scr)
        pk_scr[...] = jnp.zeros_like(pk_scr)
        pv_scr[...] = jnp.zeros_like(pv_scr)
        plr_scr[...] = jnp.zeros_like(plr_scr)

    rows = r_ref.shape[0]
    r_raw, k_raw, v_raw, l_raw = r_ref[...], k_ref[...], v_ref[...], l_ref[...]
    r = _shift_mix(r_raw, pr_scr[...], mur_ref[...])
    k = _shift_mix(k_raw, pk_scr[...], muk_ref[...])
    v = _shift_mix(v_raw, pv_scr[...], muv_ref[...])
    lo = _shift_mix(l_raw, plr_scr[...], mul_ref[...])
    pr_scr[...] = r_raw[rows - 1:rows, :]
    pk_scr[...] = k_raw[rows - 1:rows, :]
    pv_scr[...] = v_raw[rows - 1:rows, :]
    plr_scr[...] = l_raw[rows - 1:rows, :]

    wpre = w0_ref[...] + _dot(jnp.tanh(lo).astype(BF16), w2_ref[...])
    w = -jax.nn.softplus(-wpre) - 0.5
    a = jax.nn.sigmoid(a0_ref[...] + _dot(lo.astype(BF16), a2_ref[...]))
    g = _dot(jax.nn.sigmoid(lo).astype(BF16), g2_ref[...])

    kks = k * kkw_ref[...]
    sq = kks * kks
    hsum = hsum_ref[...]
    ss = jnp.concatenate([_dot(sq[:, j * LANES:(j + 1) * LANES], hsum, precision=HI)
                          for j in range(RWKV_WIDTH // LANES)], axis=-1)
    kk = kks / jnp.maximum(jnp.sqrt(ss), 1e-12)
    ro_ref[...] = r
    ko_ref[...] = k * (1.0 + (a - 1.0) * ka_ref[...])
    vo_ref[...] = v
    kko_ref[...] = kk
    bo_ref[...] = kk * a
    lw_ref[...] = -jnp.exp(w)
    go_ref[...] = g


def _rwkv_prep(p, mu, w0, w2, a0, a2, g2, kkw, ka, seq):
    m = p.shape[0]
    w = RWKV_WIDTH
    ts = min(256, seq)
    nb = seq // ts
    lora_cols = DECAY_LORA + AAA_LORA + GATE_LORA
    mu_r, mu_k, mu_v = mu[0:w], mu[w:2 * w], mu[2 * w:3 * w]
    mu_l = jnp.pad(mu[3 * w:], (0, LORA_PAD - lora_cols))
    lane_head = jnp.arange(LANES) // HEAD_DIM
    hsum = (lane_head[:, None] == lane_head[None, :]).astype(F32)
    pad_rows = lambda t, at: jnp.pad(t, ((at, LORA_PAD - at - t.shape[0]), (0, 0))).astype(BF16)
    w2 = pad_rows(w2, 0)
    a2 = pad_rows(a2, DECAY_LORA)
    g2 = pad_rows(g2, DECAY_LORA + AAA_LORA)
    row = lambda a_: a_.reshape(1, -1)
    big = pl.BlockSpec((ts, w), lambda b, i: (b * nb + i, 0))
    const = lambda shape: pl.BlockSpec(shape, lambda b, i: (0,) * len(shape))
    outs = pl.pallas_call(
        _rwkv_prep_kernel,
        grid=(m // seq, nb),
        in_specs=[pl.BlockSpec((ts, w), lambda b, i: (b * nb + i, 2)),
                  pl.BlockSpec((ts, w), lambda b, i: (b * nb + i, 3)),
                  pl.BlockSpec((ts, w), lambda b, i: (b * nb + i, 4)),
                  pl.BlockSpec((ts, LORA_PAD), lambda b, i: (b * nb + i, 10)),
                  const((1, w)), const((1, w)), const((1, w)), const((1, LORA_PAD)),
                  const((1, w)), const((LORA_PAD, w)), const((1, w)), const((LORA_PAD, w)),
                  const((LORA_PAD, w)), const((1, w)), const((1, w)), const((LANES, LANES))],
        out_specs=[big] * 7,
        out_shape=[jax.ShapeDtypeStruct((m, w), F32)] * 7,
        scratch_shapes=[pltpu.VMEM((1, w), F32), pltpu.VMEM((1, w), F32), pltpu.VMEM((1, w), F32),
                        pltpu.VMEM((1, LORA_PAD), F32)],
        compiler_params=_cparams(("arbitrary", "arbitrary"), VMEM_LIMIT),
        name="rwkv_prep",
    )(p, p, p, p, row(mu_r), row(mu_k), row(mu_v), row(mu_l),
      row(w0), w2, row(a0), a2, g2, row(kkw), row(ka), hsum)
    return outs


def _stack_heads(x):
    lane = lax.broadcasted_iota(I32, x.shape, 1)
    return jnp.concatenate([jnp.where(lane < HEAD_DIM, x, 0.0), jnp.where(lane >= HEAD_DIM, x, 0.0)], axis=0)


def _mm(a_list, b_list, dims, passes, a_exact=False):
    dn = (dims, ((), ()))
    dg = lambda x, y, **kw: lax.dot_general(x, y, dn, preferred_element_type=F32, **kw)
    if passes == 6:
        return [dg(a, b, precision=HI) for a, b in zip(a_list, b_list)]
    ah = [a.astype(BF16) for a in a_list]
    bh = [b.astype(BF16) for b in b_list]
    main = [dg(x, y) for x, y in zip(ah, bh)]
    if passes == 1:
        return main
    al = [(a - h.astype(F32)).astype(BF16) for a, h in zip(a_list, ah)]
    bl = [(b - h.astype(F32)).astype(BF16) for b, h in zip(b_list, bh)]
    c1 = [dg(x, y) for x, y in zip(ah, bl)]
    if a_exact:
        return [m + p for m, p in zip(main, c1)]
    c2 = [dg(x, y) for x, y in zip(al, bh)]
    return [m + (p + q) for m, p, q in zip(main, c1, c2)]


_NN = ((1,), (0,))
_NT = ((1,), (1,))
_TN = ((0,), (0,))
SCAN_PASSES = {"cumsum": 3, "scores": 1, "inverse": 3, "apply": 1, "state": 1, "chain": 1, "stats": 1}
SCAN_PAIRS = 8


def _unit_lower_inverse(a_list):
    n = a_list[0].shape[0]
    ri = lax.broadcasted_iota(I32, (n, n), 0)
    ci = lax.broadcasted_iota(I32, (n, n), 1)
    eye = (ri == ci).astype(F32)
    diag_blk = (ri // 16) == (ci // 16)
    mm = lambda xs, ys: _mm(xs, ys, _NN, SCAN_PASSES["inverse"])
    grow = lambda ts, ps: [t + p for t, p in zip(ts, ps)]
    d = [jnp.where(diag_blk, a, 0.0) for a in a_list]
    low = [a - x for a, x in zip(a_list, d)]
    d2 = mm(d, d)
    d4 = mm(d2, d2)
    d8 = mm(d4, d4)
    td = [eye + x for x in d]
    td = grow(td, mm(td, d2))
    td = grow(td, mm(td, d4))
    td = grow(td, mm(td, d8))
    nn = mm(td, low)
    n2 = mm(nn, nn)
    x = [eye + t for t in nn]
    x = grow(x, mm(x, n2))
    return mm(x, td)


def _scan_pairs(r, k, v, kk, bb, lw, g, rk, lng, lnb, s0):
    c = SCAN_CHUNK
    ps = SCAN_PASSES
    npair = len(r)
    ti = lax.broadcasted_iota(I32, (c, c), 0)
    si = lax.broadcasted_iota(I32, (c, c), 1)
    tril = (si <= ti).astype(F32)
    cum = _mm([tril] * npair, lw, _NN, ps["cumsum"], a_exact=True)
    wcur = [jnp.exp(x) for x in cum]
    winv = [jnp.exp(-x) for x in cum]
    wprev = [jnp.exp(x - y) for x, y in zip(cum, lw)]
    rt = [_stack_heads(x * w) for x, w in zip(r, wcur)]
    kt = [_stack_heads(x * w) for x, w in zip(k, winv)]
    bt = [_stack_heads(x * w) for x, w in zip(bb, winv)]
    at = [_stack_heads(-x * w) for x, w in zip(kk, wprev)]
    vs = [_stack_heads(x) for x in v]

    n = 2 * c
    ri = lax.broadcasted_iota(I32, (n, n), 0)
    ci = lax.broadcasted_iota(I32, (n, n), 1)
    same = (ri // c) == (ci // c)
    strict = same & (ci < ri)
    incl = same & (ci <= ri)
    sc = _mm([jnp.concatenate([x, y], axis=0) for x, y in zip(at, rt)],
             [jnp.concatenate([x, y], axis=0) for x, y in zip(bt, kt)], _NT, ps["scores"])
    a_ab = [jnp.where(strict, x[0:n, 0:n], 0.0) for x in sc]
    a_ak = [jnp.where(strict, x[0:n, n:2 * n], 0.0) for x in sc]
    a_rb = [jnp.where(incl, x[n:2 * n, 0:n], 0.0) for x in sc]
    a_rk = [jnp.where(incl, x[n:2 * n, n:2 * n], 0.0) for x in sc]

    av = _mm(a_ak, vs, _NN, ps["apply"])
    ark_v = _mm(a_rk, vs, _NN, ps["apply"])
    vk = _mm(vs, kt, _TN, ps["state"])
    tinv = _unit_lower_inverse(a_ab)
    pq = _mm(tinv, [jnp.concatenate([x, y], axis=1) for x, y in zip(at, av)], _NN, ps["apply"])
    rb = _mm(a_rb, pq, _NN, ps["apply"])
    r2 = [x + y[:, 0:LANES] for x, y in zip(rt, rb)]
    y0 = [x[:, LANES:2 * LANES] + y for x, y in zip(rb, ark_v)]
    y = [x + z for x, z in zip(_mm(r2, s0, _NT, ps["chain"]), y0)]
    eye = (ri == ci).astype(F32)
    pqb = _mm(pq, bt, _TN, ps["state"])
    wc = [w[c - 1:c, :] for w in wcur]
    m_k = [(eye + x[0:LANES]) * w for x, w in zip(pqb, wc)]
    n_k = [(x[LANES:2 * LANES] + z) * w for x, z, w in zip(pqb, vk, wc)]
    s_new = [x + z for x, z in zip(_mm(s0, m_k, _NN, ps["chain"]), n_k)]

    yp = [x[0:c, :] + x[c:n, :] for x in y]
    lane = lax.broadcasted_iota(I32, (LANES, LANES), 0) // HEAD_DIM
    lane_t = lax.broadcasted_iota(I32, (LANES, LANES), 1) // HEAD_DIM
    hsum = [(lane == lane_t).astype(F32)] * npair
    inv_n = 1.0 / HEAD_DIM
    mu = [x * inv_n for x in _mm(yp, hsum, _NN, ps["stats"])]
    bonus = _mm([x * z * w for x, z, w in zip(r, k, rk)], hsum, _NN, ps["stats"])
    dev = [x - z for x, z in zip(yp, mu)]
    var = [x * inv_n for x in _mm([x * x for x in dev], hsum, _NN, ps["stats"])]
    out = [(dv * lax.rsqrt(vr + GROUPNORM_EPS) * lg + lb + bo * vv) * gg
           for dv, vr, lg, lb, bo, vv, gg in zip(dev, var, lng, lnb, bonus, v, g)]
    return out, s_new


def _rwkv_scan_kernel(r_ref, k_ref, v_ref, kk_ref, b_ref, lw_ref, g_ref, rk_ref, lng_ref, lnb_ref,
                      o_ref, s_scr):
    @pl.when(pl.program_id(2) == 0)
    def _():
        s_scr[...] = jnp.zeros_like(s_scr)

    sls = [slice(j * LANES, (j + 1) * LANES) for j in range(SCAN_PAIRS)]
    split = lambda ref: [ref[:, sl] for sl in sls]
    out, s_new = _scan_pairs(split(r_ref), split(k_ref), split(v_ref), split(kk_ref), split(b_ref), split(lw_ref),
                             split(g_ref), split(rk_ref), split(lng_ref), split(lnb_ref),
                             [s_scr[j] for j in range(SCAN_PAIRS)])
    for j in range(SCAN_PAIRS):
        o_ref[:, sls[j]] = out[j]
        s_scr[j] = s_new[j]


def _rwkv_scan(r, k, v, kk, bvec, lw, g, rk, lng, lnb, seq):
    m = r.shape[0]
    c = SCAN_CHUNK
    nc = seq // c
    width = SCAN_PAIRS * LANES
    blk = pl.BlockSpec((c, width), lambda b, h, i: (b * nc + i, h))
    prm = pl.BlockSpec((1, width), lambda b, h, i: (0, h))
    return pl.pallas_call(
        _rwkv_scan_kernel,
        grid=(m // seq, RWKV_WIDTH // width, nc),
        in_specs=[blk] * 7 + [prm] * 3,
        out_specs=blk,
        out_shape=jax.ShapeDtypeStruct((m, RWKV_WIDTH), F32),
        scratch_shapes=[pltpu.VMEM((SCAN_PAIRS, LANES, LANES), F32)],
        compiler_params=_cparams(("arbitrary", "arbitrary", "arbitrary"), VMEM_LIMIT),
        name="rwkv_scan",
    )(r, k, v, kk, bvec, lw, g, rk.reshape(1, -1), lng.reshape(1, -1), lnb.reshape(1, -1))


def _out_proj_kernel(x_ref, ya_ref, yb_ref, wa_ref, wb_ref, gt_ref, g2_ref, sh_ref, sc_ref, x1_ref, h_ref):
    y = _dot(ya_ref[...].astype(BF16), wa_ref[...]) + _dot(yb_ref[...].astype(BF16), wb_ref[...])
    x1 = x_ref[...] + gt_ref[0] * y
    x1_ref[...] = x1
    hn = x1 * lax.rsqrt(jnp.mean(x1 * x1, axis=-1, keepdims=True) + NORM_EPS) * g2_ref[...]
    h_ref[...] = hn * (1.0 + sc_ref[0]) + sh_ref[0]


def _out_proj(xf, ya, yb, w_bf, gt, g2, sh, sc, seq):
    m, d = xf.shape
    tm = min(256, seq)
    per_b = seq // tm
    half = GMLP_WIDTH
    bspec = pl.BlockSpec((1, 1, d), lambda i: (i // per_b, 0, 0))
    return pl.pallas_call(
        _out_proj_kernel,
        grid=(m // tm,),
        in_specs=[pl.BlockSpec((tm, d), lambda i: (i, 0)),
                  pl.BlockSpec((tm, half), lambda i: (i, 0)),
                  pl.BlockSpec((tm, half), lambda i: (i, 0)),
                  pl.BlockSpec((half, d), lambda i: (0, 0)),
                  pl.BlockSpec((half, d), lambda i: (1, 0)),
                  bspec,
                  pl.BlockSpec((1, d), lambda i: (0, 0)),
                  bspec, bspec],
        out_specs=[pl.BlockSpec((tm, d), lambda i: (i, 0)), pl.BlockSpec((tm, d), lambda i: (i, 0))],
        out_shape=[jax.ShapeDtypeStruct((m, d), F32), jax.ShapeDtypeStruct((m, d), F32)],
        compiler_params=_cparams(("arbitrary",), VMEM_LIMIT),
        name="out_proj",
    )(xf, ya, yb, w_bf, w_bf, gt, g2.reshape(1, d), sh, sc)


def _q_scores_kernel(h_ref, wq_ref, keys_ref, o_ref):
    q = _dot(h_ref[...].astype(BF16), wq_ref[...]).astype(BF16)
    for hp in range(2 * PEER_HEADS):
        o_ref[hp * PEER_KEYS:(hp + 1) * PEER_KEYS, :] = _dot_nt(
            keys_ref[hp], q[:, hp * LANES:(hp + 1) * LANES])


def _q_scores(h, wq_bf, keys_bf):
    m, d = h.shape
    tm = 256
    n = wq_bf.shape[1]
    return pl.pallas_call(
        _q_scores_kernel,
        grid=(m // tm,),
        in_specs=[pl.BlockSpec((tm, d), lambda i: (i, 0)),
                  pl.BlockSpec((d, n), lambda i: (0, 0)),
                  pl.BlockSpec((2 * PEER_HEADS, PEER_KEYS, LANES), lambda i: (0, 0, 0))],
        out_specs=pl.BlockSpec((n, tm), lambda i: (0, i)),
        out_shape=jax.ShapeDtypeStruct((n, m), F32),
        compiler_params=_cparams(("arbitrary",), VMEM_LIMIT),
        name="q_scores",
    )(h, wq_bf, keys_bf)


def _extract_topk(vals, k):
    rows = vals.shape[0]
    iota = lax.broadcasted_iota(I32, vals.shape, 0)
    tops, poss = [], []
    for _ in range(k):
        m = jnp.max(vals, axis=0)
        pos = jnp.min(jnp.where(vals == m[None], iota, rows), axis=0)
        vals = jnp.where(iota == pos[None], -jnp.inf, vals)
        tops.append(m)
        poss.append(pos)
    return tops, poss


def _topk_kernel(s_ref, e_ref, g_ref, st_scr, it_scr):
    k = PEER_TOPK

    def half_body(hp, carry):
        vals = s_ref[pl.ds(pl.multiple_of(hp * PEER_KEYS, PEER_KEYS), PEER_KEYS)]
        tops, poss = _extract_topk(vals, k)
        for j in range(k):
            st_scr[hp, j] = tops[j]
            it_scr[hp, j] = poss[j]
        return carry

    lax.fori_loop(0, 2 * PEER_HEADS, half_body, 0)

    def head_body(h, carry):
        s1 = [st_scr[2 * h, j] for j in range(k)]
        s2 = [st_scr[2 * h + 1, j] for j in range(k)]
        i1 = [it_scr[2 * h, j] for j in range(k)]
        i2 = [it_scr[2 * h + 1, j] for j in range(k)]
        pairs = [(a, b) for a in range(k) for b in range(k) if (a + 1) * (b + 1) <= k]
        cand = jnp.stack([s1[a] + s2[b] for a, b in pairs], axis=0)
        tops, poss = _extract_topk(cand, k)
        exps = [jnp.exp(t - tops[0]) for t in tops]
        den = exps[0]
        for t in exps[1:]:
            den = den + t
        for j in range(k):
            flat = jnp.zeros_like(poss[j])
            for r, (a, b) in enumerate(pairs):
                flat = jnp.where(poss[j] == r, a * k + b, flat)
            pa = flat >> 4
            pb = flat & (k - 1)
            ia = jnp.zeros_like(pa)
            ib = jnp.zeros_like(pa)
            for a in range(k):
                ia = jnp.where(pa == a, i1[a], ia)
                ib = jnp.where(pb == a, i2[a], ib)
            e_ref[h * k + j] = ia * PEER_KEYS + ib
            g_ref[h * k + j] = exps[j] / den
        return carry

    lax.fori_loop(0, PEER_HEADS, head_body, 0)


def _topk(scores3):
    n, nb, _ = scores3.shape
    blk = pl.BlockSpec((N_SEL, SUBLANES, LANES), lambda i: (0, i, 0))
    return pl.pallas_call(
        _topk_kernel,
        grid=(nb // SUBLANES,),
        in_specs=[pl.BlockSpec((n, SUBLANES, LANES), lambda i: (0, i, 0))],
        out_specs=[blk, blk],
        out_shape=[jax.ShapeDtypeStruct((N_SEL, nb, LANES), I32), jax.ShapeDtypeStruct((N_SEL, nb, LANES), F32)],
        scratch_shapes=[pltpu.VMEM((2 * PEER_HEADS, PEER_TOPK, SUBLANES, LANES), F32),
                        pltpu.VMEM((2 * PEER_HEADS, PEER_TOPK, SUBLANES, LANES), I32)],
        compiler_params=_cparams(("arbitrary",), VMEM_LIMIT),
        name="topk",
    )(scores3)


def _route_kernel(e_ref, g_ref, ids_ref, gs_ref, st_ref, q_scr):
    zero = jnp.zeros((SUBLANES, LANES), I32)
    counts = [zero] * N_STRIPS
    for kk in range(N_SEL):
        strip = e_ref[kk] >> STRIP_SHIFT
        rank = zero
        for s in range(N_STRIPS):
            hit = strip == s
            rank = jnp.where(hit, counts[s], rank)
            counts[s] = counts[s] + hit.astype(I32)
        q_scr[kk] = rank
    start = zero
    starts = []
    for s in range(N_STRIPS):
        starts.append(start)
        st_ref[s] = start
        start = start + counts[s]
    st_ref[N_STRIPS] = start

    def fix(kk, carry):
        e = e_ref[kk]
        strip = e >> STRIP_SHIFT
        base = zero
        for s in range(N_STRIPS):
            base = jnp.where(strip == s, starts[s], base)
        q_scr[kk] = q_scr[kk] + base
        return carry

    lax.fori_loop(0, N_SEL, fix, 0)

    def place(q, carry):
        acc_i = zero
        acc_g = jnp.zeros((SUBLANES, LANES), F32)
        for kk in range(N_SEL):
            hit = q_scr[kk] == q
            acc_i = jnp.where(hit, e_ref[kk], acc_i)
            acc_g = jnp.where(hit, g_ref[kk], acc_g)
        ids_ref[q] = (acc_i & (STRIP - 1)) * TABLE_ROWS
        gs_ref[q] = acc_g
        return carry

    lax.fori_loop(0, N_SEL, place, 0)


def _route(e3, g3):
    _, nb, _ = e3.shape
    blk = pl.BlockSpec((N_SEL, SUBLANES, LANES), lambda i: (0, i, 0))
    sblk = pl.BlockSpec((N_STRIPS + 1, SUBLANES, LANES), lambda i: (0, i, 0))
    return pl.pallas_call(
        _route_kernel,
        grid=(nb // SUBLANES,),
        in_specs=[blk, blk],
        out_specs=[blk, blk, sblk],
        out_shape=[jax.ShapeDtypeStruct((N_SEL, nb, LANES), I32), jax.ShapeDtypeStruct((N_SEL, nb, LANES), F32),
                   jax.ShapeDtypeStruct((N_STRIPS + 1, nb, LANES), I32)],
        scratch_shapes=[pltpu.VMEM((N_SEL, SUBLANES, LANES), I32)],
        compiler_params=_cparams(("arbitrary",), VMEM_LIMIT),
        name="route",
    )(e3, g3)


def _cast_kernel(t_ref, o_ref):
    o_ref[...] = t_ref[...].astype(BF16)


def _cast_bf16(table):
    n, d = table.shape
    tr = 512
    return pl.pallas_call(
        _cast_kernel,
        grid=(n // tr,),
        in_specs=[pl.BlockSpec((tr, d), lambda i: (i, 0))],
        out_specs=pl.BlockSpec((tr, d), lambda i: (i, 0)),
        out_shape=jax.ShapeDtypeStruct((n, d), BF16),
        compiler_params=_cparams(("arbitrary",), VMEM_LIMIT),
        name="cast",
    )(table)


def _pack_kernel(u_ref, v_ref, o_ref):
    tr = u_ref.shape[0]
    words = lambda ref: pltpu.bitcast(ref[...].reshape(tr * 2 * SUBLANES, LANES), I32).reshape(tr, SUBLANES, LANES)
    o_ref[:, 0] = words(u_ref)
    o_ref[:, 1] = words(v_ref)


def _pack(u_bf, v_bf):
    n, d = u_bf.shape
    tr = 256
    tile = lambda t: t.reshape(n, d // LANES, LANES)
    blk = pl.BlockSpec((tr, d // LANES, LANES), lambda i: (i, 0, 0))
    return pl.pallas_call(
        _pack_kernel,
        grid=(n // tr,),
        in_specs=[blk, blk],
        out_specs=pl.BlockSpec((tr, 2, SUBLANES, LANES), lambda i: (i, 0, 0, 0)),
        out_shape=jax.ShapeDtypeStruct((n, 2, SUBLANES, LANES), I32),
        compiler_params=_cparams(("arbitrary",), VMEM_LIMIT),
        name="pack",
    )(tile(u_bf), tile(v_bf))


def _fold8(xs):
    sub = lax.broadcasted_iota(I32, (SUBLANES, LANES), 0)
    xs = [xs[t ^ 3] for t in range(8)]
    low = sub < 4
    t4 = []
    for i in range(4):
        keep = jnp.where(low, xs[i], xs[i + 4])
        move = jnp.where(low, xs[i + 4], xs[i])
        t4.append(keep + pltpu.roll(move, 4, 0))
    m2 = (sub & 2) != 0
    t2 = [jnp.where(m2, t4[i] + pltpu.roll(t4[i], 2, 0), t4[i + 2] + pltpu.roll(t4[i + 2], 6, 0)) for i in range(2)]
    m1 = (sub & 1) != 0
    return jnp.where(m1, t2[0] + pltpu.roll(t2[0], 1, 0), t2[1] + pltpu.roll(t2[1], 7, 0))


def _split_dot(x, w_bf):
    rows = x.shape[0]
    hi = x.astype(BF16)
    lo = (x - hi.astype(F32)).astype(BF16)
    both = _dot(jnp.concatenate([hi, lo], axis=0), w_bf)
    return both[0:rows] + both[rows:2 * rows]


def _peer_prepare(s, st_ref, gs_ref, gsr_ref, cnt_ref, idb_ref, nr_ref):
    lane = lax.broadcasted_iota(I32, (SUBLANES, LANES), 1)
    subl = lax.broadcasted_iota(I32, (SUBLANES, LANES), 0)
    nst = N_STRIPS + 1
    total = jnp.int32(0)
    for sub in range(TOK_TILE // SUB_TOK):
        cmax = jnp.int32(0)
        rolled = jnp.zeros((SUBLANES, LANES), F32)
        counts = jnp.zeros((SUBLANES, LANES), I32)
        gs8 = gs_ref[pl.ds(sub * SUB_TOK, SUB_TOK), :]
        for t in range(SUB_TOK):
            tok = sub * SUB_TOK + t
            beg = st_ref[0, 0, tok * nst + s]
            cnt = st_ref[0, 0, tok * nst + s + 1] - beg
            idb_ref[tok] = tok * N_SEL + beg
            cmax = jnp.maximum(cmax, cnt)
            row = pltpu.roll(gs8, (N_SEL - beg) & (N_SEL - 1), 1)
            rolled = jnp.where(subl == t, row, rolled)
            counts = jnp.where(subl == t, cnt, counts)
        gsr_ref[pl.ds(sub * SUB_TOK, SUB_TOK), :] = rolled
        cnt_ref[pl.ds(sub * SUB_TOK, SUB_TOK), :] = counts
        nr = jnp.maximum((cmax + ROUND_SLOTS - 1) // ROUND_SLOTS, 1)
        nr_ref[sub] = nr
        total = total + nr
    return total


def _peer_gather(sub, rd, ids_ref, gsr_ref, cnt_ref, idb_ref, h_ref, tab_ref, stage_ref):
    lane = lax.broadcasted_iota(I32, (SUBLANES, LANES), 1)
    first = rd * ROUND_SLOTS
    tok0 = pl.multiple_of(sub * SUB_TOK, SUB_TOK)
    gates = pltpu.roll(gsr_ref[pl.ds(tok0, SUB_TOK), :], (N_SEL - first) & (N_SEL - 1), 1)
    g8 = jnp.where(lane < cnt_ref[pl.ds(tok0, SUB_TOK), :] - first, gates, 0.0)
    h8 = h_ref.at[pl.ds(sub * SUB_TOK, SUB_TOK)]
    ids0 = [idb_ref[sub * SUB_TOK + t] + first for t in range(SUB_TOK)]
    h = [h8[t] for t in range(SUB_TOK)]
    red = []
    for j in range(ROUND_SLOTS):
        parts = []
        for t in range(SUB_TOK):
            row = pl.multiple_of(ids_ref[0, 0, ids0[t] + j], TABLE_ROWS)
            stage_ref[t * ROUND_SLOTS + j] = tab_ref[pl.ds(row, TABLE_ROWS), :]
            u = pltpu.bitcast(stage_ref[t * ROUND_SLOTS + j, 0:SUBLANES], BF16)
            p = (u * h[t]).astype(F32)
            parts.append(p[0:SUBLANES] + p[SUBLANES:2 * SUBLANES])
        red.append(_fold8(parts))
    return jnp.concatenate(red, axis=0), g8


def _peer_activate(red, g8):
    lane = lax.broadcasted_iota(I32, (SUBLANES, LANES), 1)
    group = lane // (LANES // ROUND_SLOTS)
    pre = jnp.zeros((SUBLANES, LANES), F32)
    for j in range(ROUND_SLOTS):
        dots = jnp.sum(red[j * SUBLANES:(j + 1) * SUBLANES], axis=-1, keepdims=True)
        pre = jnp.where(group == j, dots, pre)
    ri = lax.broadcasted_iota(I32, (LANES, LANES), 0)
    ci = lax.broadcasted_iota(I32, (LANES, LANES), 1)
    spread = (ri == ci // (LANES // ROUND_SLOTS)).astype(BF16)
    return _gelu(pre) * _split_dot(g8, spread)


def _peer_apply(psub, act, stage_ref, o_ref):
    row16 = lax.broadcasted_iota(I32, (2 * SUBLANES, LANES), 0)
    lane16 = lax.broadcasted_iota(I32, (2 * SUBLANES, LANES), 1)
    diag = (lane16 & (2 * SUBLANES - 1)) == row16
    o8 = o_ref.at[pl.ds(psub * SUB_TOK, SUB_TOK)]
    for t in range(SUB_TOK):
        a = jnp.where(diag, jnp.broadcast_to(act[t:t + 1, :], (2 * SUBLANES, LANES)), 0.0).astype(BF16)
        v8 = stage_ref[pl.ds(t * ROUND_SLOTS, ROUND_SLOTS), SUBLANES:TABLE_ROWS]
        w = pltpu.bitcast(v8.reshape(ROUND_SLOTS * SUBLANES, LANES), BF16)
        o8[t] = o8[t] + _dot(a, w)


def _peer_kernel(ids_ref, st_ref, gs_ref, h_ref, tab_ref, acc_ref, o_ref, stage_a, stage_b, stage_c, gsr_ref, cnt_ref,
                 idb_ref, nr_ref):
    s = pl.program_id(0)
    nsub = TOK_TILE // SUB_TOK
    o_ref[...] = acc_ref[...]
    total = _peer_prepare(s, st_ref, gs_ref, gsr_ref, cnt_ref, idb_ref, nr_ref)
    done_rd = N_SEL // ROUND_SLOTS

    def step(state, stage_new, stage_old):
        sub, rd, sub_mid, sub_old, red, g8, act_old = state
        act = _peer_activate(red, g8)
        _peer_apply(sub_old, act_old, stage_old, o_ref)
        live = sub < nsub
        sub_c = jnp.minimum(sub, nsub - 1)
        red, g8 = _peer_gather(sub_c, jnp.where(live, rd, done_rd), ids_ref, gsr_ref, cnt_ref, idb_ref, h_ref, tab_ref,
                               stage_new)
        last = rd + 1 >= nr_ref[sub_c]
        return (jnp.where(last, sub + 1, sub), jnp.where(last, 0, rd + 1), sub_c, sub_mid, red, g8, act)

    def body(i, state):
        for _ in range(TRIP_STEPS // 3):
            state = step(state, stage_a, stage_b)
            state = step(state, stage_b, stage_c)
            state = step(state, stage_c, stage_a)
        return state

    stage_b[...] = jnp.zeros_like(stage_b)
    stage_c[...] = jnp.zeros_like(stage_c)
    zero = jnp.int32(0)
    init = (zero, zero, zero, zero, jnp.zeros((SUB_TOK * ROUND_SLOTS, LANES), F32), jnp.zeros((SUBLANES, LANES), F32),
            jnp.zeros((SUBLANES, LANES), F32))
    lax.fori_loop(0, (total + 2 + TRIP_STEPS - 1) // TRIP_STEPS, body, init)


def _peer(ids_t, st_t, gs_t, h3, table):
    ntile = ids_t.shape[0]
    m = h3.shape[0]
    smem = lambda width: pl.BlockSpec((1, 1, width), lambda s, j: (j, 0, 0), memory_space=pltpu.SMEM)
    stage = pltpu.VMEM((SUB_TOK * ROUND_SLOTS, TABLE_ROWS, LANES), I32)
    tok_blk = pl.BlockSpec((TOK_TILE, 2 * SUBLANES, LANES), lambda s, j: (j, 0, 0))
    assert ntile >= 4, "in-place accumulation needs several token tiles between a tile's write and its next read"
    return pl.pallas_call(
        _peer_kernel,
        grid=(N_STRIPS, ntile),
        in_specs=[smem(ids_t.shape[2]), smem(TOK_TILE * (N_STRIPS + 1)),
                  pl.BlockSpec((TOK_TILE, N_SEL), lambda s, j: (j, 0)),
                  tok_blk,
                  pl.BlockSpec((None, STRIP * TABLE_ROWS, LANES), lambda s, j: (s, 0, 0), pipeline_mode=pl.Buffered(1)),
                  tok_blk],
        out_specs=tok_blk,
        out_shape=jax.ShapeDtypeStruct((m, 2 * SUBLANES, LANES), F32),
        input_output_aliases={5: 0},
        scratch_shapes=[stage, stage, stage, pltpu.VMEM((TOK_TILE, LANES), F32), pltpu.VMEM((TOK_TILE, LANES), I32),
                        pltpu.SMEM((TOK_TILE,), I32), pltpu.SMEM((TOK_TILE // SUB_TOK,), I32)],
        compiler_params=_cparams(("arbitrary", "arbitrary"), VMEM_LIMIT),
        name="peer",
    )(ids_t, st_t, gs_t, h3, table, jnp.zeros((m, 2 * SUBLANES, LANES), F32))


def _final_kernel(x_ref, p_ref, gt_ref, g_ref, o_ref):
    x2 = x_ref[...] + gt_ref[0] * p_ref[...]
    o_ref[...] = x2 * lax.rsqrt(jnp.mean(x2 * x2, axis=-1, keepdims=True) + NORM_EPS) * g_ref[...]


def _final(x1, peer_out, gt, g, seq):
    m, d = x1.shape
    tm = min(512, seq)
    per_b = seq // tm
    return pl.pallas_call(
        _final_kernel,
        grid=(m // tm,),
        in_specs=[pl.BlockSpec((tm, d), lambda i: (i, 0)),
                  pl.BlockSpec((tm, d), lambda i: (i, 0)),
                  pl.BlockSpec((1, 1, d), lambda i: (i // per_b, 0, 0)),
                  pl.BlockSpec((1, d), lambda i: (0, 0))],
        out_specs=pl.BlockSpec((tm, d), lambda i: (i, 0)),
        out_shape=jax.ShapeDtypeStruct((m, d), F32),
        compiler_params=_cparams(("arbitrary",), VMEM_LIMIT),
        name="final",
    )(x1, peer_out, gt, g.reshape(1, d))


def _token_mixing(xf, seq, sh1, sc1, gt1, norm1_g, w_in, gmlp_v_g, gmlp_v_b, gmlp_ws, gmlp_bs, gmlp_out_g,
                  rwkv_mu, rwkv_w0, rwkv_w2, rwkv_a0, rwkv_a2, rwkv_g2, rwkv_kk, rwkv_ka, rwkv_rk,
                  rwkv_ln_g, rwkv_ln_b):
    w_pad = jnp.pad(w_in, ((0, 0), (0, IN_PAD - IN_WIDTH))).astype(BF16)
    p = _in_proj(xf, norm1_g, sh1, sc1, w_pad, seq)
    y_a = _gmlp(p, gmlp_v_g, gmlp_v_b, gmlp_ws, gmlp_bs, gmlp_out_g)
    r, k, v, kk, bvec, lw, g = _rwkv_prep(p, rwkv_mu, rwkv_w0, rwkv_w2, rwkv_a0, rwkv_a2, rwkv_g2,
                                          rwkv_kk, rwkv_ka, seq)
    y_b = _rwkv_scan(r, k, v, kk, bvec, lw, g, rwkv_rk, rwkv_ln_g, rwkv_ln_b, seq)
    return y_a, y_b


def _peer_ffn(h2, peer_wq, peer_keys, peer_u, peer_v):
    m, d = h2.shape
    keys_bf = peer_keys.reshape(2 * PEER_HEADS, PEER_KEYS, LANES).astype(BF16)
    scores = _q_scores(h2, peer_wq.astype(BF16), keys_bf)
    e3, g3 = _topk(scores.reshape(scores.shape[0], m // LANES, LANES))
    ids3, gs3, st3 = _route(e3, g3)
    table = _pack(_cast_bf16(peer_u), _cast_bf16(peer_v))
    return _peer(*_peer_operands(ids3, gs3, st3, h2, table)).reshape(m, d)


def _peer_operands(ids3, gs3, st3, h2, table):
    m, d = h2.shape
    ntile = m // TOK_TILE
    tok_major = lambda a_: a_.reshape(a_.shape[0], m).T
    ids_t = jnp.pad(tok_major(ids3).reshape(ntile, 1, TOK_TILE * N_SEL), ((0, 0), (0, 0), (0, IDS_PAD)))
    st_t = tok_major(st3).reshape(ntile, 1, TOK_TILE * (N_STRIPS + 1))
    return (ids_t, st_t, tok_major(gs3), h2.astype(BF16).reshape(m, 2 * SUBLANES, LANES),
            table.reshape(N_STRIPS, STRIP * TABLE_ROWS, LANES))


def kernel(x, c, ada_w, ada_b, norm1_g, w_in, gmlp_v_g, gmlp_v_b, gmlp_ws, gmlp_bs, gmlp_out_g, rwkv_mu, rwkv_w0, rwkv_w2, rwkv_a0, rwkv_a2, rwkv_g2, rwkv_kk, rwkv_ka, rwkv_rk, rwkv_ln_g, rwkv_ln_b, w_out, norm2_g, peer_wq, peer_keys, peer_u, peer_v, final_g):
    bsz, seq, d = x.shape
    depth = ada_w.shape[0]
    xf = x.reshape(bsz * seq, d)
    for l in range(depth):
        mod = _mod(c, ada_w[l], ada_b[l])
        sh1, sc1, gt1, sh2, sc2, gt2 = [t.reshape(bsz, 1, d) for t in jnp.split(mod, 6, axis=-1)]
        y_a, y_b = _token_mixing(xf, seq, sh1, sc1, gt1, norm1_g[l], w_in[l], gmlp_v_g[l], gmlp_v_b[l],
                                 gmlp_ws[l], gmlp_bs[l], gmlp_out_g[l], rwkv_mu[l], rwkv_w0[l], rwkv_w2[l],
                                 rwkv_a0[l], rwkv_a2[l], rwkv_g2[l], rwkv_kk[l], rwkv_ka[l], rwkv_rk[l],
                                 rwkv_ln_g[l], rwkv_ln_b[l])
        x1, h2 = _out_proj(xf, y_a, y_b, w_out[l].astype(BF16), gt1, norm2_g[l], sh2, sc2, seq)
        parts = _peer_ffn(h2, peer_wq[l], peer_keys[l], peer_u[l], peer_v[l])
        last = l == depth - 1
        assert last, "multi-layer stacking needs an un-normalised residual output"
        xf = _final(x1, parts, gt2, final_g, seq)
    return xf.reshape(bsz, seq, d)
```

```python
import functools

import jax
import jax.numpy as jnp
from jax import lax
from jax.experimental import pallas as pl
from jax.experimental.pallas import tpu as pltpu

F32 = jnp.float32
BF16 = jnp.bfloat16
I32 = jnp.int32
HI = lax.Precision.HIGHEST

D_MODEL = 2048
GMLP_WIDTH = 1024
RWKV_WIDTH = 1024
GMLP_BLOCK = 128
GMLP_HEADS = 8
CHUNK = 64
HEAD_DIM = 64
DECAY_LORA = 64
AAA_LORA = 64
GATE_LORA = 160
LORA_PAD = 512
IN_WIDTH = 5408
IN_PAD = 5632
PEER_KEYS = 128
PEER_HEADS = 8
PEER_TOPK = 16
N_SEL = PEER_HEADS * PEER_TOPK
N_EXPERTS = PEER_KEYS * PEER_KEYS
N_STRIPS = 4
STRIP = N_EXPERTS // N_STRIPS
STRIP_SHIFT = STRIP.bit_length() - 1
TABLE_ROWS = 16
NORM_EPS = 1e-6
LN_EPS = 1e-5
GROUPNORM_EPS = 64e-5
SCAN_CHUNK = 64
LANES = 128
SUBLANES = 8
TOK_TILE = 256
TRIP_STEPS = 12
SUB_TOK = 8
ROUND_SLOTS = 8
IDS_PAD = 384
VMEM_LIMIT = 56 * 1024 * 1024


def _cparams(sem, vmem=None):
    return pltpu.CompilerParams(dimension_semantics=sem, vmem_limit_bytes=vmem)


def _gelu(x):
    return 0.5 * x * (1.0 + lax.erf(x * 0.7071067811865476))


def _dot(a, b, **kw):
    return jnp.dot(a, b, preferred_element_type=F32, **kw)


def _dot_nt(a, b, **kw):
    return lax.dot_general(a, b, (((1,), (1,)), ((), ())), preferred_element_type=F32, **kw)


def _dot_tn(a, b, **kw):
    return lax.dot_general(a, b, (((0,), (0,)), ((), ())), preferred_element_type=F32, **kw)


def _mod_kernel(c_ref, w_ref, b_ref, o_ref):
    c = c_ref[...]
    sc = (c * jax.nn.sigmoid(c)).astype(BF16)
    o_ref[...] = _dot(sc, w_ref[...].astype(BF16)) + b_ref[...]


def _mod(c, w, b):
    bsz, d = c.shape
    n = w.shape[1]
    tn = 1024
    return pl.pallas_call(
        _mod_kernel,
        grid=(n // tn,),
        in_specs=[pl.BlockSpec((bsz, d), lambda j: (0, 0)),
                  pl.BlockSpec((d, tn), lambda j: (0, j)),
                  pl.BlockSpec((1, tn), lambda j: (0, j))],
        out_specs=pl.BlockSpec((bsz, tn), lambda j: (0, j)),
        out_shape=jax.ShapeDtypeStruct((bsz, n), F32),
        compiler_params=_cparams(("arbitrary",), VMEM_LIMIT),
        name="mod",
    )(c, w, b.reshape(1, n))


def _in_proj_kernel(x_ref, g_ref, sh_ref, sc_ref, w_ref, o_ref, h_scr):
    @pl.when(pl.program_id(1) == 0)
    def _():
        x = x_ref[...]
        y = x * lax.rsqrt(jnp.mean(x * x, axis=-1, keepdims=True) + NORM_EPS) * g_ref[...]
        h_scr[...] = (y * (1.0 + sc_ref[0]) + sh_ref[0]).astype(BF16)

    o_ref[...] = _dot(h_scr[...], w_ref[...])


def _in_proj(xf, g, sh, sc, w_bf, seq):
    m, d = xf.shape
    n = w_bf.shape[1]
    tm, tn = min(1024, seq), 512
    per_b = seq // tm
    return pl.pallas_call(
        _in_proj_kernel,
        grid=(m // tm, n // tn),
        in_specs=[pl.BlockSpec((tm, d), lambda i, j: (i, 0)),
                  pl.BlockSpec((1, d), lambda i, j: (0, 0)),
                  pl.BlockSpec((1, 1, d), lambda i, j: (i // per_b, 0, 0)),
                  pl.BlockSpec((1, 1, d), lambda i, j: (i // per_b, 0, 0)),
                  pl.BlockSpec((d, tn), lambda i, j: (0, j))],
        out_specs=pl.BlockSpec((tm, tn), lambda i, j: (i, j)),
        out_shape=jax.ShapeDtypeStruct((m, n), F32),
        scratch_shapes=[pltpu.VMEM((tm, d), BF16)],
        compiler_params=_cparams(("arbitrary", "arbitrary"), VMEM_LIMIT),
        name="in_proj",
    )(xf, g.reshape(1, d), sh, sc, w_bf)


def _gmlp_kernel(u_ref, v_ref, vg_ref, vb_ref, ws_ref, bs_ref, og_ref, o_ref):
    gu = _gelu(u_ref[...])
    gv = _gelu(v_ref[...])
    mu = jnp.mean(gv, axis=-1, keepdims=True)
    var = jnp.mean(jnp.square(gv - mu), axis=-1, keepdims=True)
    vn = ((gv - mu) * lax.rsqrt(var + LN_EPS) * vg_ref[...] + vb_ref[...]).astype(BF16)
    ri = lax.broadcasted_iota(I32, (GMLP_BLOCK, GMLP_BLOCK), 0) // CHUNK
    ci = lax.broadcasted_iota(I32, (GMLP_BLOCK, GMLP_BLOCK), 1) // CHUNK
    keep = ci <= ri
    bs = bs_ref[...]
    parts = []
    for h in range(GMLP_HEADS):
        w = jnp.where(keep, ws_ref[h], 0.0).astype(BF16)
        z = _dot(w, vn[:, h * LANES:(h + 1) * LANES]) + bs[:, h:h + 1]
        parts.append(gu[:, h * LANES:(h + 1) * LANES] * z)
    y = jnp.concatenate(parts, axis=-1)
    o_ref[...] = y * lax.rsqrt(jnp.mean(y * y, axis=-1, keepdims=True) + NORM_EPS) * og_ref[...]


def _gmlp(p, vg, vb, ws, bs, og):
    m = p.shape[0]
    w = GMLP_WIDTH
    return pl.pallas_call(
        _gmlp_kernel,
        grid=(m // GMLP_BLOCK,),
        in_specs=[pl.BlockSpec((GMLP_BLOCK, w), lambda i: (i, 0)),
                  pl.BlockSpec((GMLP_BLOCK, w), lambda i: (i, 1)),
                  pl.BlockSpec((1, w), lambda i: (0, 0)),
                  pl.BlockSpec((1, w), lambda i: (0, 0)),
                  pl.BlockSpec((GMLP_HEADS, GMLP_BLOCK, GMLP_BLOCK), lambda i: (0, 0, 0)),
                  pl.BlockSpec((GMLP_BLOCK, GMLP_HEADS), lambda i: (0, 0)),
                  pl.BlockSpec((1, w), lambda i: (0, 0))],
        out_specs=pl.BlockSpec((GMLP_BLOCK, w), lambda i: (i, 0)),
        out_shape=jax.ShapeDtypeStruct((m, w), F32),
        compiler_params=_cparams(("arbitrary",), VMEM_LIMIT),
        name="gmlp",
    )(p, p, vg.reshape(1, w), vb.reshape(1, w), ws, bs.T, og.reshape(1, w))


def _shift_mix(p, prev_row, mu):
    rolled = pltpu.roll(p, 1, 0)
    first = lax.broadcasted_iota(I32, p.shape, 0) == 0
    shifted = jnp.where(first, prev_row, rolled)
    return p + (shifted - p) * mu


def _rwkv_prep_kernel(r_ref, k_ref, v_ref, l_ref, mur_ref, muk_ref, muv_ref, mul_ref,
                      w0_ref, w2_ref, a0_ref, a2_ref, g2_ref, kkw_ref, ka_ref, hsum_ref,
                      ro_ref, ko_ref, vo_ref, kko_ref, bo_ref, lw_ref, go_ref,
                      pr_scr, pk_scr, pv_scr, plr_scr):
    @pl.when(pl.program_id(1) == 0)
    def _():
        pr_scr[...] = jnp.zeros_like(pr_scr)
        pk_scr[...] = jnp.zeros_like(pk_scr)
        pv_scr[...] = jnp.zeros_like(pv_scr)
        plr_scr[...] = jnp.zeros_like(plr_scr)

    rows = r_ref.shape[0]
    r_raw, k_raw, v_raw, l_raw = r_ref[...], k_ref[...], v_ref[...], l_ref[...]
    r = _shift_mix(r_raw, pr_scr[...], mur_ref[...])
    k = _shift_mix(k_raw, pk_scr[...], muk_ref[...])
    v = _shift_mix(v_raw, pv_scr[...], muv_ref[...])
    lo = _shift_mix(l_raw, plr_scr[...], mul_ref[...])
    pr_scr[...] = r_raw[rows - 1:rows, :]
    pk_scr[...] = k_raw[rows - 1:rows, :]
    pv_scr[...] = v_raw[rows - 1:rows, :]
    plr_scr[...] = l_raw[rows - 1:rows, :]

    wpre = w0_ref[...] + _dot(jnp.tanh(lo).astype(BF16), w2_ref[...])
    w = -jax.nn.softplus(-wpre) - 0.5
    a = jax.nn.sigmoid(a0_ref[...] + _dot(lo.astype(BF16), a2_ref[...]))
    g = _dot(jax.nn.sigmoid(lo).astype(BF16), g2_ref[...])

    kks = k * kkw_ref[...]
    sq = kks * kks
    hsum = hsum_ref[...]
    ss = jnp.concatenate([_dot(sq[:, j * LANES:(j + 1) * LANES], hsum, precision=HI)
                          for j in range(RWKV_WIDTH // LANES)], axis=-1)
    kk = kks / jnp.maximum(jnp.sqrt(ss), 1e-12)
    ro_ref[...] = r
    ko_ref[...] = k * (1.0 + (a - 1.0) * ka_ref[...])
    vo_ref[...] = v
    kko_ref[...] = kk
    bo_ref[...] = kk * a
    lw_ref[...] = -jnp.exp(w)
    go_ref[...] = g


def _rwkv_prep(p, mu, w0, w2, a0, a2, g2, kkw, ka, seq):
    m = p.shape[0]
    w = RWKV_WIDTH
    ts = min(256, seq)
    nb = seq // ts
    lora_cols = DECAY_LORA + AAA_LORA + GATE_LORA
    mu_r, mu_k, mu_v = mu[0:w], mu[w:2 * w], mu[2 * w:3 * w]
    mu_l = jnp.pad(mu[3 * w:], (0, LORA_PAD - lora_cols))
    lane_head = jnp.arange(LANES) // HEAD_DIM
    hsum = (lane_head[:, None] == lane_head[None, :]).astype(F32)
    pad_rows = lambda t, at: jnp.pad(t, ((at, LORA_PAD - at - t.shape[0]), (0, 0))).astype(BF16)
    w2 = pad_rows(w2, 0)
    a2 = pad_rows(a2, DECAY_LORA)
    g2 = pad_rows(g2, DECAY_LORA + AAA_LORA)
    row = lambda a_: a_.reshape(1, -1)
    big = pl.BlockSpec((ts, w), lambda b, i: (b * nb + i, 0))
    const = lambda shape: pl.BlockSpec(shape, lambda b, i: (0,) * len(shape))
    outs = pl.pallas_call(
        _rwkv_prep_kernel,
        grid=(m // seq, nb),
        in_specs=[pl.BlockSpec((ts, w), lambda b, i: (b * nb + i, 2)),
                  pl.BlockSpec((ts, w), lambda b, i: (b * nb + i, 3)),
                  pl.BlockSpec((ts, w), lambda b, i: (b * nb + i, 4)),
                  pl.BlockSpec((ts, LORA_PAD), lambda b, i: (b * nb + i, 10)),
                  const((1, w)), const((1, w)), const((1, w)), const((1, LORA_PAD)),
                  const((1, w)), const((LORA_PAD, w)), const((1, w)), const((LORA_PAD, w)),
                  const((LORA_PAD, w)), const((1, w)), const((1, w)), const((LANES, LANES))],
        out_specs=[big] * 7,
        out_shape=[jax.ShapeDtypeStruct((m, w), F32)] * 7,
        scratch_shapes=[pltpu.VMEM((1, w), F32), pltpu.VMEM((1, w), F32), pltpu.VMEM((1, w), F32),
                        pltpu.VMEM((1, LORA_PAD), F32)],
        compiler_params=_cparams(("arbitrary", "arbitrary"), VMEM_LIMIT),
        name="rwkv_prep",
    )(p, p, p, p, row(mu_r), row(mu_k), row(mu_v), row(mu_l),
      row(w0), w2, row(a0), a2, g2, row(kkw), row(ka), hsum)
    return outs


def _stack_heads(x):
    lane = lax.broadcasted_iota(I32, x.shape, 1)
    return jnp.concatenate([jnp.where(lane < HEAD_DIM, x, 0.0), jnp.where(lane >= HEAD_DIM, x, 0.0)], axis=0)


def _mm(a_list, b_list, dims, passes, a_exact=False):
    dn = (dims, ((), ()))
    dg = lambda x, y, **kw: lax.dot_general(x, y, dn, preferred_element_type=F32, **kw)
    if passes == 6:
        return [dg(a, b, precision=HI) for a, b in zip(a_list, b_list)]
    ah = [a.astype(BF16) for a in a_list]
    bh = [b.astype(BF16) for b in b_list]
    main = [dg(x, y) for x, y in zip(ah, bh)]
    if passes == 1:
        return main
    al = [(a - h.astype(F32)).astype(BF16) for a, h in zip(a_list, ah)]
    bl = [(b - h.astype(F32)).astype(BF16) for b, h in zip(b_list, bh)]
    c1 = [dg(x, y) for x, y in zip(ah, bl)]
    if a_exact:
        return [m + p for m, p in zip(main, c1)]
    c2 = [dg(x, y) for x, y in zip(al, bh)]
    return [m + (p + q) for m, p, q in zip(main, c1, c2)]


_NN = ((1,), (0,))
_NT = ((1,), (1,))
_TN = ((0,), (0,))
SCAN_PASSES = {"cumsum": 3, "scores": 1, "inverse": 3, "apply": 1, "state": 1, "chain": 1, "stats": 1}
SCAN_PAIRS = 8


def _unit_lower_inverse(a_list):
    n = a_list[0].shape[0]
    ri = lax.broadcasted_iota(I32, (n, n), 0)
    ci = lax.broadcasted_iota(I32, (n, n), 1)
    eye = (ri == ci).astype(F32)
    diag_blk = (ri // 16) == (ci // 16)
    mm = lambda xs, ys: _mm(xs, ys, _NN, SCAN_PASSES["inverse"])
    grow = lambda ts, ps: [t + p for t, p in zip(ts, ps)]
    d = [jnp.where(diag_blk, a, 0.0) for a in a_list]
    low = [a - x for a, x in zip(a_list, d)]
    d2 = mm(d, d)
    d4 = mm(d2, d2)
    d8 = mm(d4, d4)
    td = [eye + x for x in d]
    td = grow(td, mm(td, d2))
    td = grow(td, mm(td, d4))
    td = grow(td, mm(td, d8))
    nn = mm(td, low)
    n2 = mm(nn, nn)
    x = [eye + t for t in nn]
    x = grow(x, mm(x, n2))
    return mm(x, td)


def _scan_pairs(r, k, v, kk, bb, lw, g, rk, lng, lnb, s0):
    c = SCAN_CHUNK
    ps = SCAN_PASSES
    npair = len(r)
    ti = lax.broadcasted_iota(I32, (c, c), 0)
    si = lax.broadcasted_iota(I32, (c, c), 1)
    tril = (si <= ti).astype(F32)
    cum = _mm([tril] * npair, lw, _NN, ps["cumsum"], a_exact=True)
    wcur = [jnp.exp(x) for x in cum]
    winv = [jnp.exp(-x) for x in cum]
    wprev = [jnp.exp(x - y) for x, y in zip(cum, lw)]
    rt = [_stack_heads(x * w) for x, w in zip(r, wcur)]
    kt = [_stack_heads(x * w) for x, w in zip(k, winv)]
    bt = [_stack_heads(x * w) for x, w in zip(bb, winv)]
    at = [_stack_heads(-x * w) for x, w in zip(kk, wprev)]
    vs = [_stack_heads(x) for x in v]

    n = 2 * c
    ri = lax.broadcasted_iota(I32, (n, n), 0)
    ci = lax.broadcasted_iota(I32, (n, n), 1)
    same = (ri // c) == (ci // c)
    strict = same & (ci < ri)
    incl = same & (ci <= ri)
    sc = _mm([jnp.concatenate([x, y], axis=0) for x, y in zip(at, rt)],
             [jnp.concatenate([x, y], axis=0) for x, y in zip(bt, kt)], _NT, ps["scores"])
    a_ab = [jnp.where(strict, x[0:n, 0:n], 0.0) for x in sc]
    a_ak = [jnp.where(strict, x[0:n, n:2 * n], 0.0) for x in sc]
    a_rb = [jnp.where(incl, x[n:2 * n, 0:n], 0.0) for x in sc]
    a_rk = [jnp.where(incl, x[n:2 * n, n:2 * n], 0.0) for x in sc]

    av = _mm(a_ak, vs, _NN, ps["apply"])
    ark_v = _mm(a_rk, vs, _NN, ps["apply"])
    vk = _mm(vs, kt, _TN, ps["state"])
    tinv = _unit_lower_inverse(a_ab)
    pq = _mm(tinv, [jnp.concatenate([x, y], axis=1) for x, y in zip(at, av)], _NN, ps["apply"])
    rb = _mm(a_rb, pq, _NN, ps["apply"])
    r2 = [x + y[:, 0:LANES] for x, y in zip(rt, rb)]
    y0 = [x[:, LANES:2 * LANES] + y for x, y in zip(rb, ark_v)]
    y = [x + z for x, z in zip(_mm(r2, s0, _NT, ps["chain"]), y0)]
    eye = (ri == ci).astype(F32)
    pqb = _mm(pq, bt, _TN, ps["state"])
    wc = [w[c - 1:c, :] for w in wcur]
    m_k = [(eye + x[0:LANES]) * w for x, w in zip(pqb, wc)]
    n_k = [(x[LANES:2 * LANES] + z) * w for x, z, w in zip(pqb, vk, wc)]
    s_new = [x + z for x, z in zip(_mm(s0, m_k, _NN, ps["chain"]), n_k)]

    yp = [x[0:c, :] + x[c:n, :] for x in y]
    lane = lax.broadcasted_iota(I32, (LANES, LANES), 0) // HEAD_DIM
    lane_t = lax.broadcasted_iota(I32, (LANES, LANES), 1) // HEAD_DIM
    hsum = [(lane == lane_t).astype(F32)] * npair
    inv_n = 1.0 / HEAD_DIM
    mu = [x * inv_n for x in _mm(yp, hsum, _NN, ps["stats"])]
    bonus = _mm([x * z * w for x, z, w in zip(r, k, rk)], hsum, _NN, ps["stats"])
    dev = [x - z for x, z in zip(yp, mu)]
    var = [x * inv_n for x in _mm([x * x for x in dev], hsum, _NN, ps["stats"])]
    out = [(dv * lax.rsqrt(vr + GROUPNORM_EPS) * lg + lb + bo * vv) * gg
           for dv, vr, lg, lb, bo, vv, gg in zip(dev, var, lng, lnb, bonus, v, g)]
    return out, s_new


def _rwkv_scan_kernel(r_ref, k_ref, v_ref, kk_ref, b_ref, lw_ref, g_ref, rk_ref, lng_ref, lnb_ref,
                      o_ref, s_scr):
    @pl.when(pl.program_id(2) == 0)
    def _():
        s_scr[...] = jnp.zeros_like(s_scr)

    sls = [slice(j * LANES, (j + 1) * LANES) for j in range(SCAN_PAIRS)]
    split = lambda ref: [ref[:, sl] for sl in sls]
    out, s_new = _scan_pairs(split(r_ref), split(k_ref), split(v_ref), split(kk_ref), split(b_ref), split(lw_ref),
                             split(g_ref), split(rk_ref), split(lng_ref), split(lnb_ref),
                             [s_scr[j] for j in range(SCAN_PAIRS)])
    for j in range(SCAN_PAIRS):
        o_ref[:, sls[j]] = out[j]
        s_scr[j] = s_new[j]


def _rwkv_scan(r, k, v, kk, bvec, lw, g, rk, lng, lnb, seq):
    m = r.shape[0]
    c = SCAN_CHUNK
    nc = seq // c
    width = SCAN_PAIRS * LANES
    blk = pl.BlockSpec((c, width), lambda b, h, i: (b * nc + i, h))
    prm = pl.BlockSpec((1, width), lambda b, h, i: (0, h))
    return pl.pallas_call(
        _rwkv_scan_kernel,
        grid=(m // seq, RWKV_WIDTH // width, nc),
        in_specs=[blk] * 7 + [prm] * 3,
        out_specs=blk,
        out_shape=jax.ShapeDtypeStruct((m, RWKV_WIDTH), F32),
        scratch_shapes=[pltpu.VMEM((SCAN_PAIRS, LANES, LANES), F32)],
        compiler_params=_cparams(("arbitrary", "arbitrary", "arbitrary"), VMEM_LIMIT),
        name="rwkv_scan",
    )(r, k, v, kk, bvec, lw, g, rk.reshape(1, -1), lng.reshape(1, -1), lnb.reshape(1, -1))


def _out_proj_kernel(x_ref, ya_ref, yb_ref, wa_ref, wb_ref, gt_ref, g2_ref, sh_ref, sc_ref, x1_ref, h_ref):
    y = _dot(ya_ref[...].astype(BF16), wa_ref[...]) + _dot(yb_ref[...].astype(BF16), wb_ref[...])
    x1 = x_ref[...] + gt_ref[0] * y
    x1_ref[...] = x1
    hn = x1 * lax.rsqrt(jnp.mean(x1 * x1, axis=-1, keepdims=True) + NORM_EPS) * g2_ref[...]
    h_ref[...] = hn * (1.0 + sc_ref[0]) + sh_ref[0]


def _out_proj(xf, ya, yb, w_bf, gt, g2, sh, sc, seq):
    m, d = xf.shape
    tm = min(256, seq)
    per_b = seq // tm
    half = GMLP_WIDTH
    bspec = pl.BlockSpec((1, 1, d), lambda i: (i // per_b, 0, 0))
    return pl.pallas_call(
        _out_proj_kernel,
        grid=(m // tm,),
        in_specs=[pl.BlockSpec((tm, d), lambda i: (i, 0)),
                  pl.BlockSpec((tm, half), lambda i: (i, 0)),
                  pl.BlockSpec((tm, half), lambda i: (i, 0)),
                  pl.BlockSpec((half, d), lambda i: (0, 0)),
                  pl.BlockSpec((half, d), lambda i: (1, 0)),
                  bspec,
                  pl.BlockSpec((1, d), lambda i: (0, 0)),
                  bspec, bspec],
        out_specs=[pl.BlockSpec((tm, d), lambda i: (i, 0)), pl.BlockSpec((tm, d), lambda i: (i, 0))],
        out_shape=[jax.ShapeDtypeStruct((m, d), F32), jax.ShapeDtypeStruct((m, d), F32)],
        compiler_params=_cparams(("arbitrary",), VMEM_LIMIT),
        name="out_proj",
    )(xf, ya, yb, w_bf, w_bf, gt, g2.reshape(1, d), sh, sc)


def _q_scores_kernel(h_ref, wq_ref, keys_ref, o_ref):
    q = _dot(h_ref[...].astype(BF16), wq_ref[...]).astype(BF16)
    for hp in range(2 * PEER_HEADS):
        o_ref[hp * PEER_KEYS:(hp + 1) * PEER_KEYS, :] = _dot_nt(
            keys_ref[hp], q[:, hp * LANES:(hp + 1) * LANES])


def _q_scores(h, wq_bf, keys_bf):
    m, d = h.shape
    tm = 256
    n = wq_bf.shape[1]
    return pl.pallas_call(
        _q_scores_kernel,
        grid=(m // tm,),
        in_specs=[pl.BlockSpec((tm, d), lambda i: (i, 0)),
                  pl.BlockSpec((d, n), lambda i: (0, 0)),
                  pl.BlockSpec((2 * PEER_HEADS, PEER_KEYS, LANES), lambda i: (0, 0, 0))],
        out_specs=pl.BlockSpec((n, tm), lambda i: (0, i)),
        out_shape=jax.ShapeDtypeStruct((n, m), F32),
        compiler_params=_cparams(("arbitrary",), VMEM_LIMIT),
        name="q_scores",
    )(h, wq_bf, keys_bf)


def _extract_topk(vals, k):
    rows = vals.shape[0]
    iota = lax.broadcasted_iota(I32, vals.shape, 0)
    tops, poss = [], []
    for _ in range(k):
        m = jnp.max(vals, axis=0)
        pos = jnp.min(jnp.where(vals == m[None], iota, rows), axis=0)
        vals = jnp.where(iota == pos[None], -jnp.inf, vals)
        tops.append(m)
        poss.append(pos)
    return tops, poss


def _topk_kernel(s_ref, e_ref, g_ref, st_scr, it_scr):
    k = PEER_TOPK

    def half_body(hp, carry):
        vals = s_ref[pl.ds(pl.multiple_of(hp * PEER_KEYS, PEER_KEYS), PEER_KEYS)]
        tops, poss = _extract_topk(vals, k)
        for j in range(k):
            st_scr[hp, j] = tops[j]
            it_scr[hp, j] = poss[j]
        return carry

    lax.fori_loop(0, 2 * PEER_HEADS, half_body, 0)

    def head_body(h, carry):
        s1 = [st_scr[2 * h, j] for j in range(k)]
        s2 = [st_scr[2 * h + 1, j] for j in range(k)]
        i1 = [it_scr[2 * h, j] for j in range(k)]
        i2 = [it_scr[2 * h + 1, j] for j in range(k)]
        pairs = [(a, b) for a in range(k) for b in range(k) if (a + 1) * (b + 1) <= k]
        cand = jnp.stack([s1[a] + s2[b] for a, b in pairs], axis=0)
        tops, poss = _extract_topk(cand, k)
        exps = [jnp.exp(t - tops[0]) for t in tops]
        den = exps[0]
        for t in exps[1:]:
            den = den + t
        for j in range(k):
            flat = jnp.zeros_like(poss[j])
            for r, (a, b) in enumerate(pairs):
                flat = jnp.where(poss[j] == r, a * k + b, flat)
            pa = flat >> 4
            pb = flat & (k - 1)
            ia = jnp.zeros_like(pa)
            ib = jnp.zeros_like(pa)
            for a in range(k):
                ia = jnp.where(pa == a, i1[a], ia)
                ib = jnp.where(pb == a, i2[a], ib)
            e_ref[h * k + j] = ia * PEER_KEYS + ib
            g_ref[h * k + j] = exps[j] / den
        return carry

    lax.fori_loop(0, PEER_HEADS, head_body, 0)


def _topk(scores3):
    n, nb, _ = scores3.shape
    blk = pl.BlockSpec((N_SEL, SUBLANES, LANES), lambda i: (0, i, 0))
    return pl.pallas_call(
        _topk_kernel,
        grid=(nb // SUBLANES,),
        in_specs=[pl.BlockSpec((n, SUBLANES, LANES), lambda i: (0, i, 0))],
        out_specs=[blk, blk],
        out_shape=[jax.ShapeDtypeStruct((N_SEL, nb, LANES), I32), jax.ShapeDtypeStruct((N_SEL, nb, LANES), F32)],
        scratch_shapes=[pltpu.VMEM((2 * PEER_HEADS, PEER_TOPK, SUBLANES, LANES), F32),
                        pltpu.VMEM((2 * PEER_HEADS, PEER_TOPK, SUBLANES, LANES), I32)],
        compiler_params=_cparams(("arbitrary",), VMEM_LIMIT),
        name="topk",
    )(scores3)


def _route_kernel(e_ref, g_ref, ids_ref, gs_ref, idb_ref, cnt_ref, nr_ref, q_scr):
    zero = jnp.zeros((SUBLANES, LANES), I32)
    counts = [zero] * N_STRIPS
    for kk in range(N_SEL):
        strip = e_ref[kk] >> STRIP_SHIFT
        rank = zero
        for s in range(N_STRIPS):
            hit = strip == s
            rank = jnp.where(hit, counts[s], rank)
            counts[s] = counts[s] + hit.astype(I32)
        q_scr[kk] = rank
    lane = lax.broadcasted_iota(I32, (SUBLANES, LANES), 1)
    tok_in_tile = (lax.broadcasted_iota(I32, (SUBLANES, LANES), 0) * LANES + lane) & (TOK_TILE - 1)
    start = zero
    starts = []
    for s in range(N_STRIPS):
        starts.append(start)
        idb_ref[s] = tok_in_tile * N_SEL + start
        cnt_ref[s] = counts[s]
        most = counts[s]
        for d in (1, 2, 4):
            other = jnp.where((lane & d) != 0, pltpu.roll(most, d, 1), pltpu.roll(most, LANES - d, 1))
            most = jnp.maximum(most, other)
        nr_ref[s] = jnp.maximum((most + (ROUND_SLOTS - 1)) >> 3, 1)
        start = start + counts[s]

    def fix(kk, carry):
        e = e_ref[kk]
        strip = e >> STRIP_SHIFT
        base = zero
        for s in range(N_STRIPS):
            base = jnp.where(strip == s, starts[s], base)
        q_scr[kk] = q_scr[kk] + base
        return carry

    lax.fori_loop(0, N_SEL, fix, 0)

    def place(q, carry):
        acc_i = zero
        acc_g = jnp.zeros((SUBLANES, LANES), F32)
        for kk in range(N_SEL):
            hit = q_scr[kk] == q
            acc_i = jnp.where(hit, e_ref[kk], acc_i)
            acc_g = jnp.where(hit, g_ref[kk], acc_g)
        ids_ref[q] = (acc_i & (STRIP - 1)) * TABLE_ROWS
        gs_ref[q] = acc_g
        return carry

    lax.fori_loop(0, N_SEL, place, 0)


def _route(e3, g3):
    _, nb, _ = e3.shape
    assert SUB_TOK == 8 and ROUND_SLOTS == 8 and (SUBLANES * LANES) % TOK_TILE == 0
    blk = pl.BlockSpec((N_SEL, SUBLANES, LANES), lambda i: (0, i, 0))
    sblk = pl.BlockSpec((N_STRIPS, SUBLANES, LANES), lambda i: (0, i, 0))
    strip_i32 = jax.ShapeDtypeStruct((N_STRIPS, nb, LANES), I32)
    return pl.pallas_call(
        _route_kernel,
        grid=(nb // SUBLANES,),
        in_specs=[blk, blk],
        out_specs=[blk, blk, sblk, sblk, sblk],
        out_shape=[jax.ShapeDtypeStruct((N_SEL, nb, LANES), I32), jax.ShapeDtypeStruct((N_SEL, nb, LANES), F32),
                   strip_i32, strip_i32, strip_i32],
        scratch_shapes=[pltpu.VMEM((N_SEL, SUBLANES, LANES), I32)],
        compiler_params=_cparams(("arbitrary",), VMEM_LIMIT),
        name="route",
    )(e3, g3)


def _pack_kernel(u_ref, v_ref, o_ref):
    tr = u_ref.shape[0]
    words = lambda ref: pltpu.bitcast(ref[...].astype(BF16).reshape(tr * 2 * SUBLANES, LANES),
                                      I32).reshape(tr, SUBLANES, LANES)
    o_ref[:, 0] = words(u_ref)
    o_ref[:, 1] = words(v_ref)


def _pack(u, v):
    n, d = u.shape
    tr = 256
    tile = lambda t: t.reshape(n, d // LANES, LANES)
    blk = pl.BlockSpec((tr, d // LANES, LANES), lambda i: (i, 0, 0))
    return pl.pallas_call(
        _pack_kernel,
        grid=(n // tr,),
        in_specs=[blk, blk],
        out_specs=pl.BlockSpec((tr, 2, SUBLANES, LANES), lambda i: (i, 0, 0, 0)),
        out_shape=jax.ShapeDtypeStruct((n, 2, SUBLANES, LANES), I32),
        compiler_params=_cparams(("arbitrary",), VMEM_LIMIT),
        name="pack",
    )(tile(u), tile(v))


def _fold8(xs):
    sub = lax.broadcasted_iota(I32, (SUBLANES, LANES), 0)
    xs = [xs[t ^ 3] for t in range(8)]
    low = sub < 4
    t4 = []
    for i in range(4):
        keep = jnp.where(low, xs[i], xs[i + 4])
        move = jnp.where(low, xs[i + 4], xs[i])
        t4.append(keep + pltpu.roll(move, 4, 0))
    m2 = (sub & 2) != 0
    t2 = [jnp.where(m2, t4[i] + pltpu.roll(t4[i], 2, 0), t4[i + 2] + pltpu.roll(t4[i + 2], 6, 0)) for i in range(2)]
    m1 = (sub & 1) != 0
    return jnp.where(m1, t2[0] + pltpu.roll(t2[0], 1, 0), t2[1] + pltpu.roll(t2[1], 7, 0))


def _split_dot(x, w_bf):
    rows = x.shape[0]
    hi = x.astype(BF16)
    lo = (x - hi.astype(F32)).astype(BF16)
    both = _dot(jnp.concatenate([hi, lo], axis=0), w_bf)
    return both[0:rows] + both[rows:2 * rows]


def _peer_prepare(gs_ref, idb_ref, nr_ref, gsr_ref):
    subl = lax.broadcasted_iota(I32, (SUBLANES, LANES), 0)
    total = jnp.int32(0)
    for sub in range(TOK_TILE // SUB_TOK):
        rolled = jnp.zeros((SUBLANES, LANES), F32)
        gs8 = gs_ref[pl.ds(sub * SUB_TOK, SUB_TOK), :]
        for t in range(SUB_TOK):
            row = pltpu.roll(gs8, (-idb_ref[0, 0, sub * SUB_TOK + t]) & (N_SEL - 1), 1)
            rolled = jnp.where(subl == t, row, rolled)
        gsr_ref[pl.ds(sub * SUB_TOK, SUB_TOK), :] = rolled
        total = total + nr_ref[0, 0, sub]
    return total


def _peer_gather(sub, rd, ids_ref, gsr_ref, cnt_ref, idb_ref, h_ref, tab_ref, stage_ref):
    lane = lax.broadcasted_iota(I32, (SUBLANES, LANES), 1)
    first = rd * ROUND_SLOTS
    tok0 = pl.multiple_of(sub * SUB_TOK, SUB_TOK)
    gates = pltpu.roll(gsr_ref[pl.ds(tok0, SUB_TOK), :], (N_SEL - first) & (N_SEL - 1), 1)
    g8 = jnp.where(lane < cnt_ref[pl.ds(tok0, SUB_TOK), :] - first, gates, 0.0)
    h8 = h_ref.at[pl.ds(sub * SUB_TOK, SUB_TOK)]
    ids0 = [idb_ref[0, 0, sub * SUB_TOK + t] + first for t in range(SUB_TOK)]
    h = [h8[t] for t in range(SUB_TOK)]
    red = []
    for j in range(ROUND_SLOTS):
        parts = []
        for t in range(SUB_TOK):
            row = pl.multiple_of(ids_ref[0, 0, ids0[t] + j], TABLE_ROWS)
            stage_ref[t * ROUND_SLOTS + j] = tab_ref[pl.ds(row, TABLE_ROWS), :]
            u = pltpu.bitcast(stage_ref[t * ROUND_SLOTS + j, 0:SUBLANES], BF16)
            p = (u * h[t]).astype(F32)
            parts.append(p[0:SUBLANES] + p[SUBLANES:2 * SUBLANES])
        red.append(_fold8(parts))
    return jnp.concatenate(red, axis=0), g8


def _peer_activate(red, g8):
    lane = lax.broadcasted_iota(I32, (SUBLANES, LANES), 1)
    group = lane // (LANES // ROUND_SLOTS)
    pre = jnp.zeros((SUBLANES, LANES), F32)
    for j in range(ROUND_SLOTS):
        dots = jnp.sum(red[j * SUBLANES:(j + 1) * SUBLANES], axis=-1, keepdims=True)
        pre = jnp.where(group == j, dots, pre)
    ri = lax.broadcasted_iota(I32, (LANES, LANES), 0)
    ci = lax.broadcasted_iota(I32, (LANES, LANES), 1)
    spread = (ri == ci // (LANES // ROUND_SLOTS)).astype(BF16)
    return _gelu(pre) * _split_dot(g8, spread)


def _peer_apply(psub, act, stage_ref, o_ref):
    row16 = lax.broadcasted_iota(I32, (2 * SUBLANES, LANES), 0)
    lane16 = lax.broadcasted_iota(I32, (2 * SUBLANES, LANES), 1)
    diag = (lane16 & (2 * SUBLANES - 1)) == row16
    o8 = o_ref.at[pl.ds(psub * SUB_TOK, SUB_TOK)]
    for t in range(SUB_TOK):
        a = jnp.where(diag, jnp.broadcast_to(act[t:t + 1, :], (2 * SUBLANES, LANES)), 0.0).astype(BF16)
        v8 = stage_ref[pl.ds(t * ROUND_SLOTS, ROUND_SLOTS), SUBLANES:TABLE_ROWS]
        w = pltpu.bitcast(v8.reshape(ROUND_SLOTS * SUBLANES, LANES), BF16)
        o8[t] = o8[t] + _dot(a, w)


def _peer_kernel(ids_ref, idb_ref, nr_ref, gs_ref, cnt_ref, h_ref, tab_ref, acc_ref, o_ref, stage_a, stage_b, stage_c,
                 gsr_ref):
    nsub = TOK_TILE // SUB_TOK
    o_ref[...] = acc_ref[...]
    total = _peer_prepare(gs_ref, idb_ref, nr_ref, gsr_ref)
    done_rd = N_SEL // ROUND_SLOTS

    def step(state, stage_new, stage_old):
        sub, rd, sub_mid, sub_old, red, g8, act_old = state
        act = _peer_activate(red, g8)
        _peer_apply(sub_old, act_old, stage_old, o_ref)
        live = sub < nsub
        sub_c = jnp.minimum(sub, nsub - 1)
        red, g8 = _peer_gather(sub_c, jnp.where(live, rd, done_rd), ids_ref, gsr_ref, cnt_ref, idb_ref, h_ref, tab_ref,
                               stage_new)
        last = rd + 1 >= nr_ref[0, 0, sub_c]
        return (jnp.where(last, sub + 1, sub), jnp.where(last, 0, rd + 1), sub_c, sub_mid, red, g8, act)

    def body(i, state):
        for _ in range(TRIP_STEPS // 3):
            state = step(state, stage_a, stage_b)
            state = step(state, stage_b, stage_c)
            state = step(state, stage_c, stage_a)
        return state

    stage_b[...] = jnp.zeros_like(stage_b)
    stage_c[...] = jnp.zeros_like(stage_c)
    zero = jnp.int32(0)
    init = (zero, zero, zero, zero, jnp.zeros((SUB_TOK * ROUND_SLOTS, LANES), F32), jnp.zeros((SUBLANES, LANES), F32),
            jnp.zeros((SUBLANES, LANES), F32))
    lax.fori_loop(0, (total + 2 + TRIP_STEPS - 1) // TRIP_STEPS, body, init)


def _peer(ids_t, idb_t, nr_t, gs_t, cnt_t, h3, table):
    ntile = ids_t.shape[0]
    m = h3.shape[0]
    smem = lambda width: pl.BlockSpec((1, 1, width), lambda s, j: (j, 0, 0), memory_space=pltpu.SMEM)
    smem_strip = lambda width: pl.BlockSpec((1, 1, width), lambda s, j: (s * ntile + j, 0, 0), memory_space=pltpu.SMEM)
    stage = pltpu.VMEM((SUB_TOK * ROUND_SLOTS, TABLE_ROWS, LANES), I32)
    tok_blk = pl.BlockSpec((TOK_TILE, 2 * SUBLANES, LANES), lambda s, j: (j, 0, 0))
    assert ntile >= 4, "in-place accumulation needs several token tiles between a tile's write and its next read"
    return pl.pallas_call(
        _peer_kernel,
        grid=(N_STRIPS, ntile),
        in_specs=[smem(ids_t.shape[2]), smem_strip(TOK_TILE), smem_strip(TOK_TILE // SUB_TOK),
                  pl.BlockSpec((TOK_TILE, N_SEL), lambda s, j: (j, 0)),
                  pl.BlockSpec((None, TOK_TILE, LANES), lambda s, j: (s, j, 0)),
                  tok_blk,
                  pl.BlockSpec((None, STRIP * TABLE_ROWS, LANES), lambda s, j: (s, 0, 0), pipeline_mode=pl.Buffered(1)),
                  tok_blk],
        out_specs=tok_blk,
        out_shape=jax.ShapeDtypeStruct((m, 2 * SUBLANES, LANES), F32),
        input_output_aliases={7: 0},
        scratch_shapes=[stage, stage, stage, pltpu.VMEM((TOK_TILE, LANES), F32)],
        compiler_params=_cparams(("arbitrary", "arbitrary"), VMEM_LIMIT),
        name="peer",
    )(ids_t, idb_t, nr_t, gs_t, cnt_t, h3, table, jnp.zeros((m, 2 * SUBLANES, LANES), F32))


def _final_kernel(x_ref, p_ref, gt_ref, g_ref, o_ref):
    x2 = x_ref[...] + gt_ref[0] * p_ref[...]
    o_ref[...] = x2 * lax.rsqrt(jnp.mean(x2 * x2, axis=-1, keepdims=True) + NORM_EPS) * g_ref[...]


def _final(x1, peer_out, gt, g, seq):
    m, d = x1.shape
    tm = min(512, seq)
    per_b = seq // tm
    return pl.pallas_call(
        _final_kernel,
        grid=(m // tm,),
        in_specs=[pl.BlockSpec((tm, d), lambda i: (i, 0)),
                  pl.BlockSpec((tm, d), lambda i: (i, 0)),
                  pl.BlockSpec((1, 1, d), lambda i: (i // per_b, 0, 0)),
                  pl.BlockSpec((1, d), lambda i: (0, 0))],
        out_specs=pl.BlockSpec((tm, d), lambda i: (i, 0)),
        out_shape=jax.ShapeDtypeStruct((m, d), F32),
        compiler_params=_cparams(("arbitrary",), VMEM_LIMIT),
        name="final",
    )(x1, peer_out, gt, g.reshape(1, d))


def _token_mixing(xf, seq, sh1, sc1, gt1, norm1_g, w_in, gmlp_v_g, gmlp_v_b, gmlp_ws, gmlp_bs, gmlp_out_g,
                  rwkv_mu, rwkv_w0, rwkv_w2, rwkv_a0, rwkv_a2, rwkv_g2, rwkv_kk, rwkv_ka, rwkv_rk,
                  rwkv_ln_g, rwkv_ln_b):
    w_pad = jnp.pad(w_in, ((0, 0), (0, IN_PAD - IN_WIDTH))).astype(BF16)
    p = _in_proj(xf, norm1_g, sh1, sc1, w_pad, seq)
    y_a = _gmlp(p, gmlp_v_g, gmlp_v_b, gmlp_ws, gmlp_bs, gmlp_out_g)
    r, k, v, kk, bvec, lw, g = _rwkv_prep(p, rwkv_mu, rwkv_w0, rwkv_w2, rwkv_a0, rwkv_a2, rwkv_g2,
                                          rwkv_kk, rwkv_ka, seq)
    y_b = _rwkv_scan(r, k, v, kk, bvec, lw, g, rwkv_rk, rwkv_ln_g, rwkv_ln_b, seq)
    return y_a, y_b


def _peer_ffn(h2, peer_wq, peer_keys, peer_u, peer_v):
    m, d = h2.shape
    keys_bf = peer_keys.reshape(2 * PEER_HEADS, PEER_KEYS, LANES).astype(BF16)
    scores = _q_scores(h2, peer_wq.astype(BF16), keys_bf)
    e3, g3 = _topk(scores.reshape(scores.shape[0], m // LANES, LANES))
    table = _pack(peer_u, peer_v)
    return _peer(*_peer_operands(*_route(e3, g3), h2, table)).reshape(m, d)


def _peer_operands(ids3, gs3, idb3, cnt3, nr3, h2, table):
    m, d = h2.shape
    ntile = m // TOK_TILE
    tok_major = lambda a_: a_.reshape(a_.shape[0], m).T
    ids_t = jnp.pad(tok_major(ids3).reshape(ntile, 1, TOK_TILE * N_SEL), ((0, 0), (0, 0), (0, IDS_PAD)))
    idb_t = idb3.reshape(N_STRIPS * ntile, 1, TOK_TILE)
    nr_t = nr3.reshape(N_STRIPS * ntile, TOK_TILE // SUB_TOK, SUB_TOK)[:, :, 0].reshape(N_STRIPS * ntile, 1, -1)
    cnt_t = jnp.broadcast_to(cnt3.reshape(N_STRIPS, m, 1), (N_STRIPS, m, LANES))
    return (ids_t, idb_t, nr_t, tok_major(gs3), cnt_t, h2.astype(BF16).reshape(m, 2 * SUBLANES, LANES),
            table.reshape(N_STRIPS, STRIP * TABLE_ROWS, LANES))


def kernel(x, c, ada_w, ada_b, norm1_g, w_in, gmlp_v_g, gmlp_v_b, gmlp_ws, gmlp_bs, gmlp_out_g, rwkv_mu, rwkv_w0, rwkv_w2, rwkv_a0, rwkv_a2, rwkv_g2, rwkv_kk, rwkv_ka, rwkv_rk, rwkv_ln_g, rwkv_ln_b, w_out, norm2_g, peer_wq, peer_keys, peer_u, peer_v, final_g):
    bsz, seq, d = x.shape
    depth = ada_w.shape[0]
    xf = x.reshape(bsz * seq, d)
    for l in range(depth):
        mod = _mod(c, ada_w[l], ada_b[l])
        sh1, sc1, gt1, sh2, sc2, gt2 = [t.reshape(bsz, 1, d) for t in jnp.split(mod, 6, axis=-1)]
        y_a, y_b = _token_mixing(xf, seq, sh1, sc1, gt1, norm1_g[l], w_in[l], gmlp_v_g[l], gmlp_v_b[l],
                                 gmlp_ws[l], gmlp_bs[l], gmlp_out_g[l], rwkv_mu[l], rwkv_w0[l], rwkv_w2[l],
                                 rwkv_a0[l], rwkv_a2[l], rwkv_g2[l], rwkv_kk[l], rwkv_ka[l], rwkv_rk[l],
                                 rwkv_ln_g[l], rwkv_ln_b[l])
        x1, h2 = _out_proj(xf, y_a, y_b, w_out[l].astype(BF16), gt1, norm2_g[l], sh2, sc2, seq)
        parts = _peer_ffn(h2, peer_wq[l], peer_keys[l], peer_u[l], peer_v[l])
        last = l == depth - 1
        assert last, "multi-layer stacking needs an un-normalised residual output"
        xf = _final(x1, parts, gt2, final_g, seq)
    return xf.reshape(bsz, seq, d)
```

```python
import functools

import jax
import jax.numpy as jnp
from jax import lax
from jax.experimental import pallas as pl
from jax.experimental.pallas import tpu as pltpu

F32 = jnp.float32
BF16 = jnp.bfloat16
I32 = jnp.int32
HI = lax.Precision.HIGHEST

D_MODEL = 2048
GMLP_WIDTH = 1024
RWKV_WIDTH = 1024
GMLP_BLOCK = 128
GMLP_HEADS = 8
CHUNK = 64
HEAD_DIM = 64
DECAY_LORA = 64
AAA_LORA = 64
GATE_LORA = 160
LORA_PAD = 512
IN_WIDTH = 5408
IN_PAD = 5632
PEER_KEYS = 128
PEER_HEADS = 8
PEER_TOPK = 16
N_SEL = PEER_HEADS * PEER_TOPK
N_EXPERTS = PEER_KEYS * PEER_KEYS
N_STRIPS = 4
STRIP = N_EXPERTS // N_STRIPS
STRIP_SHIFT = STRIP.bit_length() - 1
TABLE_ROWS = 16
NORM_EPS = 1e-6
LN_EPS = 1e-5
GROUPNORM_EPS = 64e-5
SCAN_CHUNK = 64
LANES = 128
SUBLANES = 8
TOK_TILE = 512
TRIP_STEPS = 24
SUB_TOK = 8
ROUND_SLOTS = 8
IDS_PAD = 384
VMEM_LIMIT = 56 * 1024 * 1024


def _cparams(sem, vmem=None):
    return pltpu.CompilerParams(dimension_semantics=sem, vmem_limit_bytes=vmem)


def _gelu(x):
    return 0.5 * x * (1.0 + lax.erf(x * 0.7071067811865476))


def _dot(a, b, **kw):
    return jnp.dot(a, b, preferred_element_type=F32, **kw)


def _dot_nt(a, b, **kw):
    return lax.dot_general(a, b, (((1,), (1,)), ((), ())), preferred_element_type=F32, **kw)


def _dot_tn(a, b, **kw):
    return lax.dot_general(a, b, (((0,), (0,)), ((), ())), preferred_element_type=F32, **kw)


def _mod_kernel(c_ref, w_ref, b_ref, o_ref):
    c = c_ref[...]
    sc = (c * jax.nn.sigmoid(c)).astype(BF16)
    o_ref[...] = _dot(sc, w_ref[...].astype(BF16)) + b_ref[...]


def _mod(c, w, b):
    bsz, d = c.shape
    n = w.shape[1]
    tn = 1024
    return pl.pallas_call(
        _mod_kernel,
        grid=(n // tn,),
        in_specs=[pl.BlockSpec((bsz, d), lambda j: (0, 0)),
                  pl.BlockSpec((d, tn), lambda j: (0, j)),
                  pl.BlockSpec((1, tn), lambda j: (0, j))],
        out_specs=pl.BlockSpec((bsz, tn), lambda j: (0, j)),
        out_shape=jax.ShapeDtypeStruct((bsz, n), F32),
        compiler_params=_cparams(("arbitrary",), VMEM_LIMIT),
        name="mod",
    )(c, w, b.reshape(1, n))


def _in_proj_kernel(x_ref, g_ref, sh_ref, sc_ref, w_ref, o_ref, h_scr):
    @pl.when(pl.program_id(1) == 0)
    def _():
        x = x_ref[...]
        y = x * lax.rsqrt(jnp.mean(x * x, axis=-1, keepdims=True) + NORM_EPS) * g_ref[...]
        h_scr[...] = (y * (1.0 + sc_ref[0]) + sh_ref[0]).astype(BF16)

    o_ref[...] = _dot(h_scr[...], w_ref[...])


def _in_proj(xf, g, sh, sc, w_bf, seq):
    m, d = xf.shape
    n = w_bf.shape[1]
    tm, tn = min(1024, seq), 512
    per_b = seq // tm
    return pl.pallas_call(
        _in_proj_kernel,
        grid=(m // tm, n // tn),
        in_specs=[pl.BlockSpec((tm, d), lambda i, j: (i, 0)),
                  pl.BlockSpec((1, d), lambda i, j: (0, 0)),
                  pl.BlockSpec((1, 1, d), lambda i, j: (i // per_b, 0, 0)),
                  pl.BlockSpec((1, 1, d), lambda i, j: (i // per_b, 0, 0)),
                  pl.BlockSpec((d, tn), lambda i, j: (0, j))],
        out_specs=pl.BlockSpec((tm, tn), lambda i, j: (i, j)),
        out_shape=jax.ShapeDtypeStruct((m, n), F32),
        scratch_shapes=[pltpu.VMEM((tm, d), BF16)],
        compiler_params=_cparams(("arbitrary", "arbitrary"), VMEM_LIMIT),
        name="in_proj",
    )(xf, g.reshape(1, d), sh, sc, w_bf)


def _gmlp_kernel(u_ref, v_ref, vg_ref, vb_ref, ws_ref, bs_ref, og_ref, o_ref):
    gu = _gelu(u_ref[...])
    gv = _gelu(v_ref[...])
    mu = jnp.mean(gv, axis=-1, keepdims=True)
    var = jnp.mean(jnp.square(gv - mu), axis=-1, keepdims=True)
    vn = ((gv - mu) * lax.rsqrt(var + LN_EPS) * vg_ref[...] + vb_ref[...]).astype(BF16)
    ri = lax.broadcasted_iota(I32, (GMLP_BLOCK, GMLP_BLOCK), 0) // CHUNK
    ci = lax.broadcasted_iota(I32, (GMLP_BLOCK, GMLP_BLOCK), 1) // CHUNK
    keep = ci <= ri
    bs = bs_ref[...]
    parts = []
    for h in range(GMLP_HEADS):
        w = jnp.where(keep, ws_ref[h], 0.0).astype(BF16)
        z = _dot(w, vn[:, h * LANES:(h + 1) * LANES]) + bs[:, h:h + 1]
        parts.append(gu[:, h * LANES:(h + 1) * LANES] * z)
    y = jnp.concatenate(parts, axis=-1)
    o_ref[...] = y * lax.rsqrt(jnp.mean(y * y, axis=-1, keepdims=True) + NORM_EPS) * og_ref[...]


def _gmlp(p, vg, vb, ws, bs, og):
    m = p.shape[0]
    w = GMLP_WIDTH
    return pl.pallas_call(
        _gmlp_kernel,
        grid=(m // GMLP_BLOCK,),
        in_specs=[pl.BlockSpec((GMLP_BLOCK, w), lambda i: (i, 0)),
                  pl.BlockSpec((GMLP_BLOCK, w), lambda i: (i, 1)),
                  pl.BlockSpec((1, w), lambda i: (0, 0)),
                  pl.BlockSpec((1, w), lambda i: (0, 0)),
                  pl.BlockSpec((GMLP_HEADS, GMLP_BLOCK, GMLP_BLOCK), lambda i: (0, 0, 0)),
                  pl.BlockSpec((GMLP_BLOCK, GMLP_HEADS), lambda i: (0, 0)),
                  pl.BlockSpec((1, w), lambda i: (0, 0))],
        out_specs=pl.BlockSpec((GMLP_BLOCK, w), lambda i: (i, 0)),
        out_shape=jax.ShapeDtypeStruct((m, w), F32),
        compiler_params=_cparams(("arbitrary",), VMEM_LIMIT),
        name="gmlp",
    )(p, p, vg.reshape(1, w), vb.reshape(1, w), ws, bs.T, og.reshape(1, w))


def _shift_mix(p, prev_row, mu):
    rolled = pltpu.roll(p, 1, 0)
    first = lax.broadcasted_iota(I32, p.shape, 0) == 0
    shifted = jnp.where(first, prev_row, rolled)
    return p + (shifted - p) * mu


def _rwkv_prep_kernel(r_ref, k_ref, v_ref, l_ref, mur_ref, muk_ref, muv_ref, mul_ref,
                      w0_ref, w2_ref, a0_ref, a2_ref, g2_ref, kkw_ref, ka_ref, hsum_ref,
                      ro_ref, ko_ref, vo_ref, kko_ref, bo_ref, lw_ref, go_ref,
                      pr_scr, pk_scr, pv_scr, plr_scr):
    @pl.when(pl.program_id(1) == 0)
    def _():
        pr_scr[...] = jnp.zeros_like(pr_scr)
        pk_scr[...] = jnp.zeros_like(pk_scr)
        pv_scr[...] = jnp.zeros_like(pv_scr)
        plr_scr[...] = jnp.zeros_like(plr_scr)

    rows = r_ref.shape[0]
    r_raw, k_raw, v_raw, l_raw = r_ref[...], k_ref[...], v_ref[...], l_ref[...]
    r = _shift_mix(r_raw, pr_scr[...], mur_ref[...])
    k = _shift_mix(k_raw, pk_scr[...], muk_ref[...])
    v = _shift_mix(v_raw, pv_scr[...], muv_ref[...])
    lo = _shift_mix(l_raw, plr_scr[...], mul_ref[...])
    pr_scr[...] = r_raw[rows - 1:rows, :]
    pk_scr[...] = k_raw[rows - 1:rows, :]
    pv_scr[...] = v_raw[rows - 1:rows, :]
    plr_scr[...] = l_raw[rows - 1:rows, :]

    wpre = w0_ref[...] + _dot(jnp.tanh(lo).astype(BF16), w2_ref[...])
    w = -jax.nn.softplus(-wpre) - 0.5
    a = jax.nn.sigmoid(a0_ref[...] + _dot(lo.astype(BF16), a2_ref[...]))
    g = _dot(jax.nn.sigmoid(lo).astype(BF16), g2_ref[...])

    kks = k * kkw_ref[...]
    sq = kks * kks
    hsum = hsum_ref[...]
    ss = jnp.concatenate([_dot(sq[:, j * LANES:(j + 1) * LANES], hsum, precision=HI)
                          for j in range(RWKV_WIDTH // LANES)], axis=-1)
    kk = kks / jnp.maximum(jnp.sqrt(ss), 1e-12)
    ro_ref[...] = r
    ko_ref[...] = k * (1.0 + (a - 1.0) * ka_ref[...])
    vo_ref[...] = v
    kko_ref[...] = kk
    bo_ref[...] = kk * a
    lw_ref[...] = -jnp.exp(w)
    go_ref[...] = g


def _rwkv_prep(p, mu, w0, w2, a0, a2, g2, kkw, ka, seq):
    m = p.shape[0]
    w = RWKV_WIDTH
    ts = min(256, seq)
    nb = seq // ts
    lora_cols = DECAY_LORA + AAA_LORA + GATE_LORA
    mu_r, mu_k, mu_v = mu[0:w], mu[w:2 * w], mu[2 * w:3 * w]
    mu_l = jnp.pad(mu[3 * w:], (0, LORA_PAD - lora_cols))
    lane_head = jnp.arange(LANES) // HEAD_DIM
    hsum = (lane_head[:, None] == lane_head[None, :]).astype(F32)
    pad_rows = lambda t, at: jnp.pad(t, ((at, LORA_PAD - at - t.shape[0]), (0, 0))).astype(BF16)
    w2 = pad_rows(w2, 0)
    a2 = pad_rows(a2, DECAY_LORA)
    g2 = pad_rows(g2, DECAY_LORA + AAA_LORA)
    row = lambda a_: a_.reshape(1, -1)
    big = pl.BlockSpec((ts, w), lambda b, i: (b * nb + i, 0))
    const = lambda shape: pl.BlockSpec(shape, lambda b, i: (0,) * len(shape))
    outs = pl.pallas_call(
        _rwkv_prep_kernel,
        grid=(m // seq, nb),
        in_specs=[pl.BlockSpec((ts, w), lambda b, i: (b * nb + i, 2)),
                  pl.BlockSpec((ts, w), lambda b, i: (b * nb + i, 3)),
                  pl.BlockSpec((ts, w), lambda b, i: (b * nb + i, 4)),
                  pl.BlockSpec((ts, LORA_PAD), lambda b, i: (b * nb + i, 10)),
                  const((1, w)), const((1, w)), const((1, w)), const((1, LORA_PAD)),
                  const((1, w)), const((LORA_PAD, w)), const((1, w)), const((LORA_PAD, w)),
                  const((LORA_PAD, w)), const((1, w)), const((1, w)), const((LANES, LANES))],
        out_specs=[big] * 7,
        out_shape=[jax.ShapeDtypeStruct((m, w), F32)] * 7,
        scratch_shapes=[pltpu.VMEM((1, w), F32), pltpu.VMEM((1, w), F32), pltpu.VMEM((1, w), F32),
                        pltpu.VMEM((1, LORA_PAD), F32)],
        compiler_params=_cparams(("arbitrary", "arbitrary"), VMEM_LIMIT),
        name="rwkv_prep",
    )(p, p, p, p, row(mu_r), row(mu_k), row(mu_v), row(mu_l),
      row(w0), w2, row(a0), a2, g2, row(kkw), row(ka), hsum)
    return outs


def _stack_heads(x):
    lane = lax.broadcasted_iota(I32, x.shape, 1)
    return jnp.concatenate([jnp.where(lane < HEAD_DIM, x, 0.0), jnp.where(lane >= HEAD_DIM, x, 0.0)], axis=0)


def _mm(a_list, b_list, dims, passes, a_exact=False):
    dn = (dims, ((), ()))
    dg = lambda x, y, **kw: lax.dot_general(x, y, dn, preferred_element_type=F32, **kw)
    if passes == 6:
        return [dg(a, b, precision=HI) for a, b in zip(a_list, b_list)]
    ah = [a.astype(BF16) for a in a_list]
    bh = [b.astype(BF16) for b in b_list]
    main = [dg(x, y) for x, y in zip(ah, bh)]
    if passes == 1:
        return main
    al = [(a - h.astype(F32)).astype(BF16) for a, h in zip(a_list, ah)]
    bl = [(b - h.astype(F32)).astype(BF16) for b, h in zip(b_list, bh)]
    c1 = [dg(x, y) for x, y in zip(ah, bl)]
    if a_exact:
        return [m + p for m, p in zip(main, c1)]
    c2 = [dg(x, y) for x, y in zip(al, bh)]
    return [m + (p + q) for m, p, q in zip(main, c1, c2)]


_NN = ((1,), (0,))
_NT = ((1,), (1,))
_TN = ((0,), (0,))
SCAN_PASSES = {"cumsum": 3, "scores": 1, "inverse": 3, "apply": 1, "state": 1, "chain": 1, "stats": 1}
SCAN_PAIRS = 8
INV_BLOCK = 16


def _unit_lower_inverse(a_list):
    n = a_list[0].shape[0]
    ri = lax.broadcasted_iota(I32, (n, n), 0)
    ci = lax.broadcasted_iota(I32, (n, n), 1)
    eye = (ri == ci).astype(F32)
    diag_blk = (ri // INV_BLOCK) == (ci // INV_BLOCK)
    mm = lambda xs, ys: _mm(xs, ys, _NN, SCAN_PASSES["inverse"])
    grow = lambda ts, ps: [t + p for t, p in zip(ts, ps)]
    d = [jnp.where(diag_blk, a, 0.0) for a in a_list]
    low = [a - x for a, x in zip(a_list, d)]
    d2 = mm(d, d)
    d4 = mm(d2, d2)
    d8 = mm(d4, d4)
    td = [eye + x for x in d]
    td = grow(td, mm(td, d2))
    td = grow(td, mm(td, d4))
    td = grow(td, mm(td, d8))
    nn = mm(td, low)
    n2 = mm(nn, nn)
    x = [eye + t for t in nn]
    x = grow(x, mm(x, n2))
    return mm(x, td)


def _scan_pairs(r, k, v, kk, bb, lw, g, rk, lng, lnb, s0):
    c = SCAN_CHUNK
    ps = SCAN_PASSES
    npair = len(r)
    ti = lax.broadcasted_iota(I32, (c, c), 0)
    si = lax.broadcasted_iota(I32, (c, c), 1)
    tril = (si <= ti).astype(F32)
    cum = _mm([tril] * npair, lw, _NN, ps["cumsum"], a_exact=True)
    wcur = [jnp.exp(x) for x in cum]
    winv = [jnp.exp(-x) for x in cum]
    wprev = [jnp.exp(x - y) for x, y in zip(cum, lw)]
    rt = [_stack_heads(x * w) for x, w in zip(r, wcur)]
    kt = [_stack_heads(x * w) for x, w in zip(k, winv)]
    bt = [_stack_heads(x * w) for x, w in zip(bb, winv)]
    at = [_stack_heads(-x * w) for x, w in zip(kk, wprev)]
    vs = [_stack_heads(x) for x in v]

    n = 2 * c
    ri = lax.broadcasted_iota(I32, (n, n), 0)
    ci = lax.broadcasted_iota(I32, (n, n), 1)
    same = (ri // c) == (ci // c)
    strict = same & (ci < ri)
    incl = same & (ci <= ri)
    sc = _mm([jnp.concatenate([x, y], axis=0) for x, y in zip(at, rt)],
             [jnp.concatenate([x, y], axis=0) for x, y in zip(bt, kt)], _NT, ps["scores"])
    a_ab = [jnp.where(strict, x[0:n, 0:n], 0.0) for x in sc]
    a_ak = [jnp.where(strict, x[0:n, n:2 * n], 0.0) for x in sc]
    a_rb = [jnp.where(incl, x[n:2 * n, 0:n], 0.0) for x in sc]
    a_rk = [jnp.where(incl, x[n:2 * n, n:2 * n], 0.0) for x in sc]

    av = _mm(a_ak, vs, _NN, ps["apply"])
    ark_v = _mm(a_rk, vs, _NN, ps["apply"])
    vk = _mm(vs, kt, _TN, ps["state"])
    tinv = _unit_lower_inverse(a_ab)
    pq = _mm(tinv, [jnp.concatenate([x, y], axis=1) for x, y in zip(at, av)], _NN, ps["apply"])
    rb = _mm(a_rb, pq, _NN, ps["apply"])
    r2 = [x + y[:, 0:LANES] for x, y in zip(rt, rb)]
    y0 = [x[:, LANES:2 * LANES] + y for x, y in zip(rb, ark_v)]
    y = [x + z for x, z in zip(_mm(r2, s0, _NT, ps["chain"]), y0)]
    eye = (ri == ci).astype(F32)
    pqb = _mm(pq, bt, _TN, ps["state"])
    wc = [w[c - 1:c, :] for w in wcur]
    m_k = [(eye + x[0:LANES]) * w for x, w in zip(pqb, wc)]
    n_k = [(x[LANES:2 * LANES] + z) * w for x, z, w in zip(pqb, vk, wc)]
    s_new = [x + z for x, z in zip(_mm(s0, m_k, _NN, ps["chain"]), n_k)]

    yp = [x[0:c, :] + x[c:n, :] for x in y]
    lane = lax.broadcasted_iota(I32, (LANES, LANES), 0) // HEAD_DIM
    lane_t = lax.broadcasted_iota(I32, (LANES, LANES), 1) // HEAD_DIM
    hsum = [(lane == lane_t).astype(F32)] * npair
    inv_n = 1.0 / HEAD_DIM
    mu = [x * inv_n for x in _mm(yp, hsum, _NN, ps["stats"])]
    bonus = _mm([x * z * w for x, z, w in zip(r, k, rk)], hsum, _NN, ps["stats"])
    dev = [x - z for x, z in zip(yp, mu)]
    var = [x * inv_n for x in _mm([x * x for x in dev], hsum, _NN, ps["stats"])]
    out = [(dv * lax.rsqrt(vr + GROUPNORM_EPS) * lg + lb + bo * vv) * gg
           for dv, vr, lg, lb, bo, vv, gg in zip(dev, var, lng, lnb, bonus, v, g)]
    return out, s_new


def _rwkv_scan_kernel(r_ref, k_ref, v_ref, kk_ref, b_ref, lw_ref, g_ref, rk_ref, lng_ref, lnb_ref,
                      o_ref, s_scr):
    @pl.when(pl.program_id(2) == 0)
    def _():
        s_scr[...] = jnp.zeros_like(s_scr)

    sls = [slice(j * LANES, (j + 1) * LANES) for j in range(SCAN_PAIRS)]
    split = lambda ref: [ref[:, sl] for sl in sls]
    out, s_new = _scan_pairs(split(r_ref), split(k_ref), split(v_ref), split(kk_ref), split(b_ref), split(lw_ref),
                             split(g_ref), split(rk_ref), split(lng_ref), split(lnb_ref),
                             [s_scr[j] for j in range(SCAN_PAIRS)])
    for j in range(SCAN_PAIRS):
        o_ref[:, sls[j]] = out[j]
        s_scr[j] = s_new[j]


def _rwkv_scan(r, k, v, kk, bvec, lw, g, rk, lng, lnb, seq):
    m = r.shape[0]
    c = SCAN_CHUNK
    nc = seq // c
    width = SCAN_PAIRS * LANES
    blk = pl.BlockSpec((c, width), lambda b, h, i: (b * nc + i, h))
    prm = pl.BlockSpec((1, width), lambda b, h, i: (0, h))
    return pl.pallas_call(
        _rwkv_scan_kernel,
        grid=(m // seq, RWKV_WIDTH // width, nc),
        in_specs=[blk] * 7 + [prm] * 3,
        out_specs=blk,
        out_shape=jax.ShapeDtypeStruct((m, RWKV_WIDTH), F32),
        scratch_shapes=[pltpu.VMEM((SCAN_PAIRS, LANES, LANES), F32)],
        compiler_params=_cparams(("arbitrary", "arbitrary", "arbitrary"), VMEM_LIMIT),
        name="rwkv_scan",
    )(r, k, v, kk, bvec, lw, g, rk.reshape(1, -1), lng.reshape(1, -1), lnb.reshape(1, -1))


def _out_proj_kernel(x_ref, ya_ref, yb_ref, wa_ref, wb_ref, gt_ref, g2_ref, sh_ref, sc_ref, x1_ref, h_ref):
    y = _dot(ya_ref[...].astype(BF16), wa_ref[...]) + _dot(yb_ref[...].astype(BF16), wb_ref[...])
    x1 = x_ref[...] + gt_ref[0] * y
    x1_ref[...] = x1
    hn = x1 * lax.rsqrt(jnp.mean(x1 * x1, axis=-1, keepdims=True) + NORM_EPS) * g2_ref[...]
    h_ref[...] = (hn * (1.0 + sc_ref[0]) + sh_ref[0]).astype(BF16)


def _out_proj(xf, ya, yb, w_bf, gt, g2, sh, sc, seq):
    m, d = xf.shape
    tm = min(256, seq)
    per_b = seq // tm
    half = GMLP_WIDTH
    bspec = pl.BlockSpec((1, 1, d), lambda i: (i // per_b, 0, 0))
    return pl.pallas_call(
        _out_proj_kernel,
        grid=(m // tm,),
        in_specs=[pl.BlockSpec((tm, d), lambda i: (i, 0)),
                  pl.BlockSpec((tm, half), lambda i: (i, 0)),
                  pl.BlockSpec((tm, half), lambda i: (i, 0)),
                  pl.BlockSpec((half, d), lambda i: (0, 0)),
                  pl.BlockSpec((half, d), lambda i: (1, 0)),
                  bspec,
                  pl.BlockSpec((1, d), lambda i: (0, 0)),
                  bspec, bspec],
        out_specs=[pl.BlockSpec((tm, d), lambda i: (i, 0)), pl.BlockSpec((tm, d), lambda i: (i, 0))],
        out_shape=[jax.ShapeDtypeStruct((m, d), F32), jax.ShapeDtypeStruct((m, d), BF16)],
        compiler_params=_cparams(("arbitrary",), VMEM_LIMIT),
        name="out_proj",
    )(xf, ya, yb, w_bf, w_bf, gt, g2.reshape(1, d), sh, sc)


def _q_scores_kernel(h_ref, wq_ref, keys_ref, o_ref):
    q = _dot(h_ref[...], wq_ref[...]).astype(BF16)
    for hp in range(2 * PEER_HEADS):
        o_ref[hp * PEER_KEYS:(hp + 1) * PEER_KEYS, :] = _dot_nt(
            keys_ref[hp], q[:, hp * LANES:(hp + 1) * LANES])


def _q_scores(h, wq_bf, keys_bf):
    m, d = h.shape
    tm = 256
    n = wq_bf.shape[1]
    return pl.pallas_call(
        _q_scores_kernel,
        grid=(m // tm,),
        in_specs=[pl.BlockSpec((tm, d), lambda i: (i, 0)),
                  pl.BlockSpec((d, n), lambda i: (0, 0)),
                  pl.BlockSpec((2 * PEER_HEADS, PEER_KEYS, LANES), lambda i: (0, 0, 0))],
        out_specs=pl.BlockSpec((n, tm), lambda i: (0, i)),
        out_shape=jax.ShapeDtypeStruct((n, m), F32),
        compiler_params=_cparams(("arbitrary",), VMEM_LIMIT),
        name="q_scores",
    )(h, wq_bf, keys_bf)


def _extract_topk(vals, k):
    rows = vals.shape[0]
    iota = lax.broadcasted_iota(I32, vals.shape, 0)
    tops, poss = [], []
    for _ in range(k):
        m = jnp.max(vals, axis=0)
        pos = jnp.min(jnp.where(vals == m[None], iota, rows), axis=0)
        vals = jnp.where(iota == pos[None], -jnp.inf, vals)
        tops.append(m)
        poss.append(pos)
    return tops, poss


def _topk_kernel(s_ref, e_ref, g_ref, st_scr, it_scr):
    k = PEER_TOPK

    def half_body(hp, carry):
        vals = s_ref[pl.ds(pl.multiple_of(hp * PEER_KEYS, PEER_KEYS), PEER_KEYS)]
        tops, poss = _extract_topk(vals, k)
        for j in range(k):
            st_scr[hp, j] = tops[j]
            it_scr[hp, j] = poss[j]
        return carry

    lax.fori_loop(0, 2 * PEER_HEADS, half_body, 0)

    def head_body(h, carry):
        s1 = [st_scr[2 * h, j] for j in range(k)]
        s2 = [st_scr[2 * h + 1, j] for j in range(k)]
        i1 = [it_scr[2 * h, j] for j in range(k)]
        i2 = [it_scr[2 * h + 1, j] for j in range(k)]
        pairs = [(a, b) for a in range(k) for b in range(k) if (a + 1) * (b + 1) <= k]
        cand = jnp.stack([s1[a] + s2[b] for a, b in pairs], axis=0)
        tops, poss = _extract_topk(cand, k)
        exps = [jnp.exp(t - tops[0]) for t in tops]
        den = exps[0]
        for t in exps[1:]:
            den = den + t
        for j in range(k):
            flat = jnp.zeros_like(poss[j])
            for r, (a, b) in enumerate(pairs):
                flat = jnp.where(poss[j] == r, a * k + b, flat)
            pa = flat >> (k.bit_length() - 1)
            pb = flat & (k - 1)
            ia = jnp.zeros_like(pa)
            ib = jnp.zeros_like(pa)
            for a in range(k):
                ia = jnp.where(pa == a, i1[a], ia)
                ib = jnp.where(pb == a, i2[a], ib)
            e_ref[h * k + j] = ia * PEER_KEYS + ib
            g_ref[h * k + j] = exps[j] / den
        return carry

    lax.fori_loop(0, PEER_HEADS, head_body, 0)


def _topk(scores3):
    n, nb, _ = scores3.shape
    blk = pl.BlockSpec((N_SEL, SUBLANES, LANES), lambda i: (0, i, 0))
    return pl.pallas_call(
        _topk_kernel,
        grid=(nb // SUBLANES,),
        in_specs=[pl.BlockSpec((n, SUBLANES, LANES), lambda i: (0, i, 0))],
        out_specs=[blk, blk],
        out_shape=[jax.ShapeDtypeStruct((N_SEL, nb, LANES), I32), jax.ShapeDtypeStruct((N_SEL, nb, LANES), F32)],
        scratch_shapes=[pltpu.VMEM((2 * PEER_HEADS, PEER_TOPK, SUBLANES, LANES), F32),
                        pltpu.VMEM((2 * PEER_HEADS, PEER_TOPK, SUBLANES, LANES), I32)],
        compiler_params=_cparams(("arbitrary",), VMEM_LIMIT),
        name="topk",
    )(scores3)


def _route_kernel(e_ref, g_ref, ids_ref, gs_ref, idb_ref, cnt_ref, nr_ref, q_scr):
    zero = jnp.zeros((SUBLANES, LANES), I32)
    counts = [zero] * N_STRIPS
    for kk in range(N_SEL):
        strip = e_ref[kk] >> STRIP_SHIFT
        rank = zero
        for s in range(N_STRIPS):
            hit = strip == s
            rank = jnp.where(hit, counts[s], rank)
            counts[s] = counts[s] + hit.astype(I32)
        q_scr[kk] = rank
    lane = lax.broadcasted_iota(I32, (SUBLANES, LANES), 1)
    tok_in_tile = (lax.broadcasted_iota(I32, (SUBLANES, LANES), 0) * LANES + lane) & (TOK_TILE - 1)
    start = zero
    starts = []
    for s in range(N_STRIPS):
        starts.append(start)
        idb_ref[s] = tok_in_tile * N_SEL + start
        cnt_ref[s] = counts[s]
        most = counts[s]
        for d in (1, 2, 4):
            other = jnp.where((lane & d) != 0, pltpu.roll(most, d, 1), pltpu.roll(most, LANES - d, 1))
            most = jnp.maximum(most, other)
        nr_ref[s] = jnp.maximum((most + (ROUND_SLOTS - 1)) >> (ROUND_SLOTS.bit_length() - 1), 1)
        start = start + counts[s]

    def fix(kk, carry):
        e = e_ref[kk]
        strip = e >> STRIP_SHIFT
        base = zero
        for s in range(N_STRIPS):
            base = jnp.where(strip == s, starts[s], base)
        q_scr[kk] = q_scr[kk] + base
        return carry

    lax.fori_loop(0, N_SEL, fix, 0)

    def place(q, carry):
        acc_i = zero
        acc_g = jnp.zeros((SUBLANES, LANES), F32)
        for kk in range(N_SEL):
            hit = q_scr[kk] == q
            acc_i = jnp.where(hit, e_ref[kk], acc_i)
            acc_g = jnp.where(hit, g_ref[kk], acc_g)
        ids_ref[q] = (acc_i & (STRIP - 1)) * TABLE_ROWS
        gs_ref[q] = acc_g
        return carry

    lax.fori_loop(0, N_SEL, place, 0)


def _route(e3, g3):
    _, nb, _ = e3.shape
    assert SUB_TOK == 8 and ROUND_SLOTS == 8 and (SUBLANES * LANES) % TOK_TILE == 0
    blk = pl.BlockSpec((N_SEL, SUBLANES, LANES), lambda i: (0, i, 0))
    sblk = pl.BlockSpec((N_STRIPS, SUBLANES, LANES), lambda i: (0, i, 0))
    strip_i32 = jax.ShapeDtypeStruct((N_STRIPS, nb, LANES), I32)
    return pl.pallas_call(
        _route_kernel,
        grid=(nb // SUBLANES,),
        in_specs=[blk, blk],
        out_specs=[blk, blk, sblk, sblk, sblk],
        out_shape=[jax.ShapeDtypeStruct((N_SEL, nb, LANES), I32), jax.ShapeDtypeStruct((N_SEL, nb, LANES), F32),
                   strip_i32, strip_i32, strip_i32],
        scratch_shapes=[pltpu.VMEM((N_SEL, SUBLANES, LANES), I32)],
        compiler_params=_cparams(("arbitrary",), VMEM_LIMIT),
        name="route",
    )(e3, g3)


def _pack_kernel(u_ref, v_ref, o_ref):
    tr = u_ref.shape[0]
    words = lambda ref: pltpu.bitcast(ref[...].astype(BF16).reshape(tr * 2 * SUBLANES, LANES),
                                      I32).reshape(tr, SUBLANES, LANES)
    o_ref[:, 0] = words(u_ref)
    o_ref[:, 1] = words(v_ref)


def _pack(u, v):
    n, d = u.shape
    tr = 256
    tile = lambda t: t.reshape(n, d // LANES, LANES)
    blk = pl.BlockSpec((tr, d // LANES, LANES), lambda i: (i, 0, 0))
    return pl.pallas_call(
        _pack_kernel,
        grid=(n // tr,),
        in_specs=[blk, blk],
        out_specs=pl.BlockSpec((tr, 2, SUBLANES, LANES), lambda i: (i, 0, 0, 0)),
        out_shape=jax.ShapeDtypeStruct((n, 2, SUBLANES, LANES), I32),
        compiler_params=_cparams(("arbitrary",), VMEM_LIMIT),
        name="pack",
    )(tile(u), tile(v))


def _fold8(xs):
    sub = lax.broadcasted_iota(I32, (SUBLANES, LANES), 0)
    xs = [xs[t ^ 3] for t in range(8)]
    low = sub < 4
    t4 = []
    for i in range(4):
        keep = jnp.where(low, xs[i], xs[i + 4])
        move = jnp.where(low, xs[i + 4], xs[i])
        t4.append(keep + pltpu.roll(move, 4, 0))
    m2 = (sub & 2) != 0
    t2 = [jnp.where(m2, t4[i] + pltpu.roll(t4[i], 2, 0), t4[i + 2] + pltpu.roll(t4[i + 2], 6, 0)) for i in range(2)]
    m1 = (sub & 1) != 0
    return jnp.where(m1, t2[0] + pltpu.roll(t2[0], 1, 0), t2[1] + pltpu.roll(t2[1], 7, 0))


def _split_dot(x, w_bf):
    rows = x.shape[0]
    hi = x.astype(BF16)
    lo = (x - hi.astype(F32)).astype(BF16)
    both = _dot(jnp.concatenate([hi, lo], axis=0), w_bf)
    return both[0:rows] + both[rows:2 * rows]


def _peer_prepare(gs_ref, idb_ref, nr_ref, gsr_ref):
    subl = lax.broadcasted_iota(I32, (SUBLANES, LANES), 0)
    total = jnp.int32(0)
    for sub in range(TOK_TILE // SUB_TOK):
        rolled = jnp.zeros((SUBLANES, LANES), F32)
        gs8 = gs_ref[pl.ds(sub * SUB_TOK, SUB_TOK), :]
        for t in range(SUB_TOK):
            row = pltpu.roll(gs8, (-idb_ref[0, 0, sub * SUB_TOK + t]) & (N_SEL - 1), 1)
            rolled = jnp.where(subl == t, row, rolled)
        gsr_ref[pl.ds(sub * SUB_TOK, SUB_TOK), :] = rolled
        total = total + nr_ref[0, 0, sub]
    return total


def _peer_gather(sub, rd, ids_ref, gsr_ref, cnt_ref, idb_ref, h_ref, tab_ref, stage_ref):
    lane = lax.broadcasted_iota(I32, (SUBLANES, LANES), 1)
    first = rd * ROUND_SLOTS
    tok0 = pl.multiple_of(sub * SUB_TOK, SUB_TOK)
    gates = pltpu.roll(gsr_ref[pl.ds(tok0, SUB_TOK), :], (N_SEL - first) & (N_SEL - 1), 1)
    g8 = jnp.where(lane < cnt_ref[pl.ds(tok0, SUB_TOK), :] - first, gates, 0.0)
    h8 = h_ref.at[pl.ds(sub * SUB_TOK, SUB_TOK)]
    ids0 = [idb_ref[0, 0, sub * SUB_TOK + t] + first for t in range(SUB_TOK)]
    h = [h8[t] for t in range(SUB_TOK)]
    red = []
    for j in range(ROUND_SLOTS):
        parts = []
        for t in range(SUB_TOK):
            row = pl.multiple_of(ids_ref[0, 0, ids0[t] + j], TABLE_ROWS)
            stage_ref[t * ROUND_SLOTS + j] = tab_ref[pl.ds(row, TABLE_ROWS), :]
            u = pltpu.bitcast(stage_ref[t * ROUND_SLOTS + j, 0:SUBLANES], BF16)
            p = (u * h[t]).astype(F32)
            parts.append(p[0:SUBLANES] + p[SUBLANES:2 * SUBLANES])
        red.append(_fold8(parts))
    return jnp.concatenate(red, axis=0), g8


def _peer_activate(red, g8):
    lane = lax.broadcasted_iota(I32, (SUBLANES, LANES), 1)
    group = lane // (LANES // ROUND_SLOTS)
    pre = jnp.zeros((SUBLANES, LANES), F32)
    for j in range(ROUND_SLOTS):
        dots = jnp.sum(red[j * SUBLANES:(j + 1) * SUBLANES], axis=-1, keepdims=True)
        pre = jnp.where(group == j, dots, pre)
    ri = lax.broadcasted_iota(I32, (LANES, LANES), 0)
    ci = lax.broadcasted_iota(I32, (LANES, LANES), 1)
    spread = (ri == ci // (LANES // ROUND_SLOTS)).astype(BF16)
    return _gelu(pre) * _split_dot(g8, spread)


def _peer_apply(psub, act, stage_ref, o_ref):
    row16 = lax.broadcasted_iota(I32, (2 * SUBLANES, LANES), 0)
    lane16 = lax.broadcasted_iota(I32, (2 * SUBLANES, LANES), 1)
    diag = (lane16 & (2 * SUBLANES - 1)) == row16
    o8 = o_ref.at[pl.ds(psub * SUB_TOK, SUB_TOK)]
    for t in range(SUB_TOK):
        a = jnp.where(diag, jnp.broadcast_to(act[t:t + 1, :], (2 * SUBLANES, LANES)), 0.0).astype(BF16)
        v8 = stage_ref[pl.ds(t * ROUND_SLOTS, ROUND_SLOTS), SUBLANES:TABLE_ROWS]
        w = pltpu.bitcast(v8.reshape(ROUND_SLOTS * SUBLANES, LANES), BF16)
        o8[t] = o8[t] + _dot(a, w)


def _peer_kernel(first_strip, ids_ref, idb_ref, nr_ref, gs_ref, cnt_ref, h_ref, tab_ref, *rest):
    nsub = TOK_TILE // SUB_TOK
    if first_strip:
        o_ref, stage_a, stage_b, stage_c, gsr_ref = rest
        o_ref[...] = jnp.zeros_like(o_ref)
    else:
        acc_ref, o_ref, stage_a, stage_b, stage_c, gsr_ref = rest
        o_ref[...] = acc_ref[...]
    total = _peer_prepare(gs_ref, idb_ref, nr_ref, gsr_ref)
    done_rd = N_SEL // ROUND_SLOTS

    def step(state, stage_new, stage_old):
        sub, rd, sub_mid, sub_old, red, g8, act_old = state
        act = _peer_activate(red, g8)
        _peer_apply(sub_old, act_old, stage_old, o_ref)
        live = sub < nsub
        sub_c = jnp.minimum(sub, nsub - 1)
        red, g8 = _peer_gather(sub_c, jnp.where(live, rd, done_rd), ids_ref, gsr_ref, cnt_ref, idb_ref, h_ref, tab_ref,
                               stage_new)
        last = rd + 1 >= nr_ref[0, 0, sub_c]
        return (jnp.where(last, sub + 1, sub), jnp.where(last, 0, rd + 1), sub_c, sub_mid, red, g8, act)

    def body(i, state):
        for _ in range(TRIP_STEPS // 3):
            state = step(state, stage_a, stage_b)
            state = step(state, stage_b, stage_c)
            state = step(state, stage_c, stage_a)
        return state

    stage_b[...] = jnp.zeros_like(stage_b)
    stage_c[...] = jnp.zeros_like(stage_c)
    zero = jnp.int32(0)
    init = (zero, zero, zero, zero, jnp.zeros((SUB_TOK * ROUND_SLOTS, LANES), F32), jnp.zeros((SUBLANES, LANES), F32),
            jnp.zeros((SUBLANES, LANES), F32))
    lax.fori_loop(0, (total + 2 + TRIP_STEPS - 1) // TRIP_STEPS, body, init)


def _peer(ids_t, idb_t, nr_t, gs_t, cnt_t, h3, table):
    ntile = ids_t.shape[0]
    m = h3.shape[0]
    stage = pltpu.VMEM((SUB_TOK * ROUND_SLOTS, TABLE_ROWS, LANES), I32)
    tok_blk = pl.BlockSpec((TOK_TILE, 2 * SUBLANES, LANES), lambda j: (j, 0, 0))
    acc = None
    for s in range(N_STRIPS):
        smem = lambda width: pl.BlockSpec((1, 1, width), lambda j: (j, 0, 0), memory_space=pltpu.SMEM)
        smem_strip = lambda width: pl.BlockSpec((1, 1, width), lambda j, s=s: (s * ntile + j, 0, 0),
                                                memory_space=pltpu.SMEM)
        in_specs = [smem(ids_t.shape[2]), smem_strip(TOK_TILE), smem_strip(TOK_TILE // SUB_TOK),
                    pl.BlockSpec((TOK_TILE, N_SEL), lambda j: (j, 0)),
                    pl.BlockSpec((None, TOK_TILE, LANES), lambda j, s=s: (s, j, 0)),
                    tok_blk,
                    pl.BlockSpec((None, STRIP * TABLE_ROWS, LANES), lambda j, s=s: (s, 0, 0),
                                 pipeline_mode=pl.Buffered(1))]
        operands = [ids_t, idb_t, nr_t, gs_t, cnt_t, h3, table]
        if acc is not None:
            in_specs.append(tok_blk)
            operands.append(acc)
        acc = pl.pallas_call(
            functools.partial(_peer_kernel, acc is None),
            grid=(ntile,),
            in_specs=in_specs,
            out_specs=tok_blk,
            out_shape=jax.ShapeDtypeStruct((m, 2 * SUBLANES, LANES), F32),
            scratch_shapes=[stage, stage, stage, pltpu.VMEM((TOK_TILE, LANES), F32)],
            compiler_params=_cparams(("arbitrary",), VMEM_LIMIT),
            name="peer",
        )(*operands)
    return acc


def _final_kernel(x_ref, p_ref, gt_ref, g_ref, o_ref):
    x2 = x_ref[...] + gt_ref[0] * p_ref[...]
    o_ref[...] = x2 * lax.rsqrt(jnp.mean(x2 * x2, axis=-1, keepdims=True) + NORM_EPS) * g_ref[...]


def _final(x1, peer_out, gt, g, seq):
    m, d = x1.shape
    tm = min(512, seq)
    per_b = seq // tm
    return pl.pallas_call(
        _final_kernel,
        grid=(m // tm,),
        in_specs=[pl.BlockSpec((tm, d), lambda i: (i, 0)),
                  pl.BlockSpec((tm, d), lambda i: (i, 0)),
                  pl.BlockSpec((1, 1, d), lambda i: (i // per_b, 0, 0)),
                  pl.BlockSpec((1, d), lambda i: (0, 0))],
        out_specs=pl.BlockSpec((tm, d), lambda i: (i, 0)),
        out_shape=jax.ShapeDtypeStruct((m, d), F32),
        compiler_params=_cparams(("arbitrary",), VMEM_LIMIT),
        name="final",
    )(x1, peer_out, gt, g.reshape(1, d))


def _token_mixing(xf, seq, sh1, sc1, gt1, norm1_g, w_in, gmlp_v_g, gmlp_v_b, gmlp_ws, gmlp_bs, gmlp_out_g,
                  rwkv_mu, rwkv_w0, rwkv_w2, rwkv_a0, rwkv_a2, rwkv_g2, rwkv_kk, rwkv_ka, rwkv_rk,
                  rwkv_ln_g, rwkv_ln_b):
    w_pad = jnp.pad(w_in, ((0, 0), (0, IN_PAD - IN_WIDTH))).astype(BF16)
    p = _in_proj(xf, norm1_g, sh1, sc1, w_pad, seq)
    y_a = _gmlp(p, gmlp_v_g, gmlp_v_b, gmlp_ws, gmlp_bs, gmlp_out_g)
    r, k, v, kk, bvec, lw, g = _rwkv_prep(p, rwkv_mu, rwkv_w0, rwkv_w2, rwkv_a0, rwkv_a2, rwkv_g2,
                                          rwkv_kk, rwkv_ka, seq)
    y_b = _rwkv_scan(r, k, v, kk, bvec, lw, g, rwkv_rk, rwkv_ln_g, rwkv_ln_b, seq)
    return y_a, y_b


def _peer_ffn(h2, peer_wq, peer_keys, peer_u, peer_v):
    m, d = h2.shape
    keys_bf = peer_keys.reshape(2 * PEER_HEADS, PEER_KEYS, LANES).astype(BF16)
    scores = _q_scores(h2, peer_wq.astype(BF16), keys_bf)
    e3, g3 = _topk(scores.reshape(scores.shape[0], m // LANES, LANES))
    table = _pack(peer_u, peer_v)
    return _peer(*_peer_operands(*_route(e3, g3), h2, table)).reshape(m, d)


def _peer_operands(ids3, gs3, idb3, cnt3, nr3, h2, table):
    m, d = h2.shape
    ntile = m // TOK_TILE
    tok_major = lambda a_: a_.reshape(a_.shape[0], m).T
    ids_t = jnp.pad(tok_major(ids3).reshape(ntile, 1, TOK_TILE * N_SEL), ((0, 0), (0, 0), (0, IDS_PAD)))
    idb_t = idb3.reshape(N_STRIPS * ntile, 1, TOK_TILE)
    nr_t = nr3.reshape(N_STRIPS * ntile, TOK_TILE // SUB_TOK, SUB_TOK)[:, :, 0].reshape(N_STRIPS * ntile, 1, -1)
    cnt_t = jnp.broadcast_to(cnt3.reshape(N_STRIPS, m, 1), (N_STRIPS, m, LANES))
    return (ids_t, idb_t, nr_t, tok_major(gs3), cnt_t, h2.reshape(m, 2 * SUBLANES, LANES),
            table.reshape(N_STRIPS, STRIP * TABLE_ROWS, LANES))


def kernel(x, c, ada_w, ada_b, norm1_g, w_in, gmlp_v_g, gmlp_v_b, gmlp_ws, gmlp_bs, gmlp_out_g, rwkv_mu, rwkv_w0, rwkv_w2, rwkv_a0, rwkv_a2, rwkv_g2, rwkv_kk, rwkv_ka, rwkv_rk, rwkv_ln_g, rwkv_ln_b, w_out, norm2_g, peer_wq, peer_keys, peer_u, peer_v, final_g):
    bsz, seq, d = x.shape
    depth = ada_w.shape[0]
    xf = x.reshape(bsz * seq, d)
    for l in range(depth):
        mod = _mod(c, ada_w[l], ada_b[l])
        sh1, sc1, gt1, sh2, sc2, gt2 = [t.reshape(bsz, 1, d) for t in jnp.split(mod, 6, axis=-1)]
        y_a, y_b = _token_mixing(xf, seq, sh1, sc1, gt1, norm1_g[l], w_in[l], gmlp_v_g[l], gmlp_v_b[l],
                                 gmlp_ws[l], gmlp_bs[l], gmlp_out_g[l], rwkv_mu[l], rwkv_w0[l], rwkv_w2[l],
                                 rwkv_a0[l], rwkv_a2[l], rwkv_g2[l], rwkv_kk[l], rwkv_ka[l], rwkv_rk[l],
                                 rwkv_ln_g[l], rwkv_ln_b[l])
        x1, h2 = _out_proj(xf, y_a, y_b, w_out[l].astype(BF16), gt1, norm2_g[l], sh2, sc2, seq)
        parts = _peer_ffn(h2, peer_wq[l], peer_keys[l], peer_u[l], peer_v[l])
        last = l == depth - 1
        assert last, "multi-layer stacking needs an un-normalised residual output"
        xf = _final(x1, parts, gt2, final_g, seq)
    return xf.reshape(bsz, seq, d)
```
